```python
import jax, jax.numpy as jnp
from jax import lax
import numpy as np

D_MODEL = 4096
BATCH = 4
SEQ = 4096
DEPTH = 4

PLE_DIM = 256
POOL_WINDOWS = (2, 4, 8, 16)
POOL_WIDTH = D_MODEL // 4
POOL_GROUP = POOL_WIDTH // len(POOL_WINDOWS)
SGU_WIDTH = D_MODEL // 4
SGU_HEADS = 8
SGU_HEAD_DIM = SGU_WIDTH // SGU_HEADS
CHUNK = 128
SB_WIDTH = D_MODEL - POOL_WIDTH - SGU_WIDTH
SB_HEAD_DIM = 128
SB_HEADS = SB_WIDTH // SB_HEAD_DIM
Q_BLOCK = 128
MIX_WIDTH = POOL_WIDTH + SGU_WIDTH + SB_WIDTH
IN_WIDTH = POOL_WIDTH + 2 * SGU_WIDTH + 3 * SB_WIDTH
D_FF = ((8 * D_MODEL // 3 + 255) // 256) * 256
RMS_EPS = 1e-6
LN_EPS = 1e-5

kernel_name = 'hybrid_pool_sgu_stickbreak_block'


def rms_norm(x, w):
    xf = x.astype(jnp.float32)
    y = xf * lax.rsqrt(jnp.mean(xf * xf, axis=-1, keepdims=True) + RMS_EPS)
    return (y * w.astype(jnp.float32)).astype(x.dtype)


def layer_norm(x, w):
    xf = x.astype(jnp.float32)
    xc = xf - jnp.mean(xf, axis=-1, keepdims=True)
    var = jnp.mean(xc * xc, axis=-1, keepdims=True)
    return (xc * lax.rsqrt(var + LN_EPS) * w.astype(jnp.float32)).astype(x.dtype)


def pool_mixer(a, pool_w, pool_scale):
    B, S, _ = a.shape
    wmax = max(POOL_WINDOWS)
    af = a.astype(jnp.float32)
    cs = jnp.cumsum(jnp.pad(af, ((0, 0), (wmax, 0), (0, 0))), axis=1)
    pos = jnp.arange(S)
    outs = []
    for g, w in enumerate(POOL_WINDOWS):
        lo, hi = g * POOL_GROUP, (g + 1) * POOL_GROUP
        csg = cs[..., lo:hi]
        win_sum = csg[:, wmax:] - csg[:, wmax - w: wmax - w + S]
        cnt = jnp.minimum(pos + 1, w).astype(jnp.float32)[None, :, None]
        outs.append(win_sum / cnt - af[..., lo:hi])
    pooled = jnp.stack(outs, axis=2).astype(a.dtype)
    y = jnp.einsum('bsgc,gcd->bsgd', pooled, pool_w).reshape(B, S, POOL_WIDTH)
    return y * pool_scale


def sgu_mixer(uv, norm_w, w_s, b_s):
    B, S, _ = uv.shape
    uv = jax.nn.gelu(uv)
    u, v = uv[..., :SGU_WIDTH], uv[..., SGU_WIDTH:]
    v = layer_norm(v, norm_w)
    nc = S // CHUNK
    v = v.reshape(B, nc, CHUNK, SGU_HEADS, SGU_HEAD_DIM)
    mask = jnp.tril(jnp.ones((CHUNK, CHUNK), dtype=bool))
    w_causal = jnp.where(mask, w_s, 0)
    mixed = jnp.einsum('hts,bnshd->bnthd', w_causal, v) + b_s.T[None, None, :, :, None]
    return u * mixed.reshape(B, S, SGU_WIDTH)


def stick_breaking_attention(q, k, v):
    B, S, H, Dh = q.shape
    scale = Dh ** -0.5
    nb = S // Q_BLOCK
    qb = q.reshape(B, nb, Q_BLOCK, H, Dh).transpose(1, 0, 2, 3, 4)
    key_pos = jnp.arange(S)

    def block(args):
        q_blk, blk_idx = args
        q_pos = blk_idx * Q_BLOCK + jnp.arange(Q_BLOCK)
        z = jnp.einsum('bthd,bshd->bhts', q_blk, k, preferred_element_type=jnp.float32) * scale
        causal = key_pos[None, :] < q_pos[:, None]
        log_beta = jax.nn.log_sigmoid(z)
        log_1m_beta = jnp.where(causal, -jax.nn.softplus(z), 0.0)
        suffix = lax.cumsum(log_1m_beta, axis=3, reverse=True) - log_1m_beta
        attn = jnp.where(causal, jnp.exp(log_beta + suffix), 0.0)
        return jnp.einsum('bhts,bshd->bthd', attn.astype(v.dtype), v)

    out = lax.map(block, (qb, jnp.arange(nb)))
    return out.transpose(1, 0, 2, 3, 4).reshape(B, S, H * Dh)


def setup_inputs(seed: int = 0) -> dict:
    key = jax.random.key(seed)
    ks = jax.random.split(key, 20)
    L = DEPTH

    def nrm(k, shape, fan_in):
        return jax.random.normal(k, shape, jnp.float32) * (fan_in ** -0.5)

    def gain(k, shape):
        return 1.0 + 0.05 * jax.random.normal(k, shape, jnp.float32)

    return {
        'x': jax.random.normal(ks[0], (BATCH, SEQ, D_MODEL), jnp.float32),
        'p': jax.random.normal(ks[1], (DEPTH, BATCH, SEQ, PLE_DIM), jnp.float32),
        'norm_mix_w': gain(ks[2], (L, D_MODEL)),
        'w_in': nrm(ks[3], (L, D_MODEL, IN_WIDTH), D_MODEL),
        'pool_w': nrm(ks[4], (L, len(POOL_WINDOWS), POOL_GROUP, POOL_GROUP), POOL_GROUP),
        'pool_scale': gain(ks[5], (L, POOL_WIDTH)),
        'sgu_norm_w': gain(ks[6], (L, SGU_WIDTH)),
        'sgu_w': nrm(ks[7], (L, SGU_HEADS, CHUNK, CHUNK), CHUNK),
        'sgu_b': 1.0 + 0.1 * jax.random.normal(ks[8], (L, SGU_HEADS, CHUNK), jnp.float32),
        'w_out': nrm(ks[9], (L, MIX_WIDTH, D_MODEL), MIX_WIDTH),
        'norm_ffn_w': gain(ks[10], (L, D_MODEL)),
        'w_gate': nrm(ks[11], (L, D_MODEL, D_FF), D_MODEL),
        'w_up': nrm(ks[12], (L, D_MODEL, D_FF), D_MODEL),
        'w_down': nrm(ks[13], (L, D_FF, D_MODEL), D_FF),
        'norm_ple_w': gain(ks[14], (L, D_MODEL)),
        'ple_gate_down': nrm(ks[15], (L, D_MODEL, PLE_DIM), D_MODEL),
        'ple_gate_up': nrm(ks[16], (L, PLE_DIM, D_MODEL), PLE_DIM),
        'ple_proj': nrm(ks[17], (L, PLE_DIM, D_MODEL), PLE_DIM),
        'final_norm_w': gain(ks[18], (D_MODEL,)),
    }


def reference(x, p, norm_mix_w, w_in, pool_w, pool_scale, sgu_norm_w, sgu_w, sgu_b,
              w_out, norm_ffn_w, w_gate, w_up, w_down, norm_ple_w, ple_gate_down,
              ple_gate_up, ple_proj, final_norm_w):
    B, S, _ = x.shape
    o_pool = POOL_WIDTH
    o_sgu = o_pool + 2 * SGU_WIDTH
    h = x
    for i in range(DEPTH):
        xn = rms_norm(h, norm_mix_w[i])
        z = xn @ w_in[i]
        y_pool = pool_mixer(z[..., :o_pool], pool_w[i], pool_scale[i])
        y_sgu = sgu_mixer(z[..., o_pool:o_sgu], sgu_norm_w[i], sgu_w[i], sgu_b[i])
        q, k, v = jnp.split(z[..., o_sgu:], 3, axis=-1)
        q = q.reshape(B, S, SB_HEADS, SB_HEAD_DIM)
        k = k.reshape(B, S, SB_HEADS, SB_HEAD_DIM)
        v = v.reshape(B, S, SB_HEADS, SB_HEAD_DIM)
        y_sb = stick_breaking_attention(q, k, v)
        y = jnp.concatenate([y_pool, y_sgu, y_sb], axis=-1)
        h = h + y @ w_out[i]
        hn = rms_norm(h, norm_ffn_w[i])
        h = h + (jax.nn.silu(hn @ w_gate[i]) * (hn @ w_up[i])) @ w_down[i]
        gn = rms_norm(h, norm_ple_w[i])
        gate = jax.nn.sigmoid((gn @ ple_gate_down[i]) @ ple_gate_up[i])
        h = h + gate * (p[i].astype(h.dtype) @ ple_proj[i])
    return rms_norm(h, final_norm_w)
```

```python
import functools
import math

import jax
import jax.numpy as jnp
from jax import lax
from jax.experimental import pallas as pl
from jax.experimental.pallas import tpu as pltpu

F32 = jnp.float32
BF16 = jnp.bfloat16

RMS_EPS = 1e-6
LN_EPS = 1e-5
POOL_WINDOWS = (2, 4, 8, 16)
POOL_HALO = 16
HEAD_DIM = 128
LANES = 128
MIB = 1 << 20
V7X_VMEM_BYTES = 64 * MIB
VMEM_LIMIT_CAP = V7X_VMEM_BYTES - 4 * MIB
INTERNAL_SCRATCH_BYTES = 6 * MIB
LOG2E = math.log2(math.e)
UNDERFLOW_BITS = 150.0


def _tile(n, target, align):
    if n <= target:
        return n
    t = (target // align) * align
    while t >= align:
        if n % t == 0:
            return t
        t -= align
    raise ValueError(f"no tile for n={n} target={target} align={align}")


def _nbytes(shape, dtype):
    return math.prod(shape) * jnp.dtype(dtype).itemsize


def _params(semantics, pipelined_bytes, resident_bytes=0):
    need = 2 * pipelined_bytes + resident_bytes + INTERNAL_SCRATCH_BYTES
    return pltpu.CompilerParams(
        dimension_semantics=semantics,
        vmem_limit_bytes=int(min(VMEM_LIMIT_CAP, max(need, 16 * MIB))),
    )


def _rms_scale(x):
    return lax.rsqrt(jnp.mean(x * x, axis=-1, keepdims=True) + RMS_EPS)


def _rmsnorm_kernel(x_ref, g_ref, o_ref):
    x = x_ref[...]
    o_ref[...] = (x * _rms_scale(x) * g_ref[...]).astype(o_ref.dtype)


def _rmsnorm(x, g):
    m, d = x.shape
    tm = _tile(m, 512, 8)
    return pl.pallas_call(
        _rmsnorm_kernel,
        grid=(m // tm,),
        in_specs=[pl.BlockSpec((tm, d), lambda i: (i, 0)), pl.BlockSpec((1, d), lambda i: (0, 0))],
        out_specs=pl.BlockSpec((tm, d), lambda i: (i, 0)),
        out_shape=jax.ShapeDtypeStruct((m, d), BF16),
        compiler_params=_params(("parallel",), _nbytes((tm, d), F32) + _nbytes((tm, d), BF16),
                                _nbytes((tm, d), F32)),
        name="rmsnorm",
    )(x, g)


def _matmul_kernel(a_ref, w_ref, o_ref):
    o_ref[...] = jnp.dot(a_ref[...], w_ref[...], preferred_element_type=F32).astype(o_ref.dtype)


def _in_proj_rows(xn, w, col0, ncols):
    m, k = xn.shape
    tm = _tile(m, 1024, 8)
    tn = _tile(math.gcd(ncols, col0) if col0 else ncols, 1024, LANES)
    off = col0 // tn
    blocks = _nbytes((tm, k), BF16) + _nbytes((k, tn), BF16) + _nbytes((tm, tn), BF16)
    return pl.pallas_call(
        _matmul_kernel,
        grid=(m // tm, ncols // tn),
        in_specs=[pl.BlockSpec((tm, k), lambda i, j: (i, 0)),
                  pl.BlockSpec((k, tn), lambda i, j: (0, j + off))],
        out_specs=pl.BlockSpec((tm, tn), lambda i, j: (i, j)),
        out_shape=jax.ShapeDtypeStruct((m, ncols), BF16),
        compiler_params=_params(("parallel", "arbitrary"), blocks, _nbytes((tm, tn), F32)),
        name="in_proj_rows",
    )(xn, w)


def _matmul_heads_kernel(a_ref, w_ref, o_ref, *, heads_per_tile):
    res = jnp.dot(a_ref[...], w_ref[...], preferred_element_type=F32)
    for hh in range(heads_per_tile):
        o_ref[hh] = res[:, hh * HEAD_DIM:(hh + 1) * HEAD_DIM].astype(o_ref.dtype)


def _in_proj_heads(xn, w, col0, ncols, batch, seq):
    m, k = xn.shape
    tm = _tile(seq, 1024, 8)
    tn = _tile(math.gcd(ncols, col0), 1024, LANES)
    off = col0 // tn
    hpt = tn // HEAD_DIM
    spt = seq // tm
    blocks = _nbytes((tm, k), BF16) + _nbytes((k, tn), BF16) + _nbytes((tm, tn), BF16)
    return pl.pallas_call(
        functools.partial(_matmul_heads_kernel, heads_per_tile=hpt),
        grid=(m // tm, ncols // tn),
        in_specs=[pl.BlockSpec((tm, k), lambda i, j: (i, 0)),
                  pl.BlockSpec((k, tn), lambda i, j: (0, j + off))],
        out_specs=pl.BlockSpec((None, hpt, tm, HEAD_DIM), lambda i, j: (i // spt, j, i % spt, 0)),
        out_shape=jax.ShapeDtypeStruct((batch, ncols // HEAD_DIM, seq, HEAD_DIM), BF16),
        compiler_params=_params(("parallel", "arbitrary"), blocks, _nbytes((tm, tn), F32)),
        name="in_proj_heads",
    )(xn, w)


def _pool_kernel(a_ref, halo_ref, pw_ref, ps_ref, o_ref, ext_ref, *, seq, tm, group):
    t0 = lax.rem(pl.program_id(0) * tm, seq)
    a = a_ref[...].astype(F32)
    ext_ref[0:POOL_HALO, :] = jnp.where(t0 == 0, 0.0, halo_ref[...].astype(F32))
    ext_ref[POOL_HALO:, :] = a
    pos = t0 + lax.broadcasted_iota(jnp.int32, (tm, 1), 0)
    for g, w in enumerate(POOL_WINDOWS):
        cols = slice(g * group, (g + 1) * group)
        win = a[:, cols]
        for back in range(1, w):
            win = win + ext_ref[POOL_HALO - back:POOL_HALO - back + tm, cols]
        cnt = jnp.minimum(pos + 1, w).astype(F32)
        pooled = win / cnt - a[:, cols]
        y = jnp.dot(pooled.astype(BF16), pw_ref[g], preferred_element_type=F32)
        o_ref[:, cols] = (y * ps_ref[:, cols]).astype(o_ref.dtype)


def _pool_mixer(z_rows, pool_w, pool_scale, seq):
    m = z_rows.shape[0]
    n_groups, group, _ = pool_w.shape
    width = n_groups * group
    tm = _tile(seq, 512, POOL_HALO)
    hpt = tm // POOL_HALO
    blocks = (_nbytes((tm, width), BF16) * 2 + _nbytes((POOL_HALO, width), BF16)
              + _nbytes(pool_w.shape, BF16) + _nbytes((1, width), F32))
    return pl.pallas_call(
        functools.partial(_pool_kernel, seq=seq, tm=tm, group=group),
        grid=(m // tm,),
        in_specs=[pl.BlockSpec((tm, width), lambda i: (i, 0)),
                  pl.BlockSpec((POOL_HALO, width), lambda i: (jnp.maximum(i * hpt - 1, 0), 0)),
                  pl.BlockSpec(pool_w.shape, lambda i: (0, 0, 0)),
                  pl.BlockSpec((1, width), lambda i: (0, 0))],
        out_specs=pl.BlockSpec((tm, width), lambda i: (i, 0)),
        out_shape=jax.ShapeDtypeStruct((m, width), BF16),
        scratch_shapes=[pltpu.VMEM((tm + POOL_HALO, width), F32)],
        compiler_params=_params(("parallel",), blocks, 4 * _nbytes((tm + POOL_HALO, width), F32)),
        name="pool_mixer",
    )(z_rows, z_rows, pool_w, pool_scale)


def _gelu_tanh(x):
    return x * (0.5 * (1.0 + jnp.tanh(math.sqrt(2.0 / math.pi) * (x + 0.044715 * (x * x * x)))))


def _sgu_kernel(u_ref, v_ref, nw_ref, ws_ref, b_ref, o_ref, *, heads, chunks):
    v = _gelu_tanh(v_ref[...].astype(F32))
    vc = v - jnp.mean(v, axis=-1, keepdims=True)
    var = jnp.mean(vc * vc, axis=-1, keepdims=True)
    vn = (vc * lax.rsqrt(var + LN_EPS) * nw_ref[...]).astype(BF16)
    row = lax.broadcasted_iota(jnp.int32, (HEAD_DIM, HEAD_DIM), 0)
    col = lax.broadcasted_iota(jnp.int32, (HEAD_DIM, HEAD_DIM), 1)
    causal = row >= col
    for h in range(heads):
        cols = slice(h * HEAD_DIM, (h + 1) * HEAD_DIM)
        w = jnp.where(causal, ws_ref[h], 0.0).astype(BF16)
        bias = b_ref[:, h:h + 1]
        for c in range(chunks):
            rows = slice(c * HEAD_DIM, (c + 1) * HEAD_DIM)
            mixed = jnp.dot(w, vn[rows, cols], preferred_element_type=F32) + bias
            u = _gelu_tanh(u_ref[rows, cols].astype(F32))
            o_ref[rows, cols] = (u * mixed).astype(o_ref.dtype)


def _sgu_mixer(z_rows, col0, norm_w, w_s, b_t):
    m = z_rows.shape[0]
    heads = w_s.shape[0]
    width = heads * HEAD_DIM
    assert col0 % width == 0 and w_s.shape[1:] == (HEAD_DIM, HEAD_DIM)
    cb = col0 // width
    tm = _tile(m, 256, HEAD_DIM)
    blocks = (3 * _nbytes((tm, width), BF16) + _nbytes((1, width), F32) + _nbytes(w_s.shape, F32)
              + _nbytes((HEAD_DIM, LANES), F32))
    return pl.pallas_call(
        functools.partial(_sgu_kernel, heads=heads, chunks=tm // HEAD_DIM),
        grid=(m // tm,),
        in_specs=[pl.BlockSpec((tm, width), lambda i: (i, cb)),
                  pl.BlockSpec((tm, width), lambda i: (i, cb + 1)),
                  pl.BlockSpec((1, width), lambda i: (0, 0)),
                  pl.BlockSpec(w_s.shape, lambda i: (0, 0, 0)),
                  pl.BlockSpec(b_t.shape, lambda i: (0, 0))],
        out_specs=pl.BlockSpec((tm, width), lambda i: (i, 0)),
        out_shape=jax.ShapeDtypeStruct((m, width), BF16),
        compiler_params=_params(("parallel",), blocks, 4 * _nbytes((tm, width), F32)),
        name="sgu_mixer",
    )(z_rows, z_rows, norm_w, w_s, b_t)


def _neg_abs(x):
    bits = lax.bitcast_convert_type(x, jnp.uint32) | jnp.uint32(0x80000000)
    return lax.bitcast_convert_type(bits, F32)


def _bf16_split(x):
    bits = lax.bitcast_convert_type(x, jnp.uint32) & jnp.uint32(0xFFFF0000)
    hi = lax.bitcast_convert_type(bits, F32)
    return hi, x - hi


def _sb_window(q, kwin, vwin, n, diag, carry, strict, cumsum_rhs, c2):
    s = lax.dot_general(q, kwin, (((1,), (1,)), ((), ())), preferred_element_type=F32)
    z2 = s * c2
    sp = jnp.maximum(z2, 0.0) + jnp.log(1.0 + jnp.exp2(_neg_abs(z2))) * LOG2E
    log_beta = z2 - sp
    inner, totals = [], []
    for j in range(n):
        spj = sp[:, j * LANES:(j + 1) * LANES]
        if diag and j == n - 1:
            spj = jnp.where(strict, spj, 0.0)
        hi, lo = _bf16_split(spj)
        lhs = jnp.concatenate([hi.astype(BF16), lo.astype(BF16)], axis=1)
        both = jnp.dot(lhs, cumsum_rhs, preferred_element_type=F32)
        inner.append(both[:, :LANES])
        totals.append(both[:, LANES:])
    weights = [None] * n
    for j in reversed(range(n)):
        suffix = inner[j] if carry is None else inner[j] + carry
        pj = jnp.exp2(log_beta[:, j * LANES:(j + 1) * LANES] - suffix)
        if diag and j == n - 1:
            pj = jnp.where(strict, pj, 0.0)
        weights[j] = pj.astype(BF16)
        carry = totals[j] if carry is None else carry + totals[j]
    p = weights[0] if n == 1 else jnp.concatenate(weights, axis=1)
    return jnp.dot(p, vwin, preferred_element_type=F32), carry


def _attn_kernel(q_ref, k_ref, v_ref, o_ref, *, heads, nq, group, lead, c2):
    row = lax.broadcasted_iota(jnp.int32, (LANES, LANES), 0)
    col = lax.broadcasted_iota(jnp.int32, (LANES, LANES), 1)
    strict = row > col
    half = jnp.concatenate([strict.astype(BF16), jnp.ones((LANES, LANES), BF16)], axis=1)
    cumsum_rhs = jnp.concatenate([half, half], axis=0)
    window = functools.partial(_sb_window, strict=strict, cumsum_rhs=cumsum_rhs, c2=c2)

    def rows(blk, n=1):
        if isinstance(blk, int):
            return slice(blk * LANES, (blk + n) * LANES)
        return pl.ds(pl.multiple_of(blk * LANES, LANES), n * LANES)

    def run_blocks(blocks):
        state = []
        for hh in range(heads):
            for qi, n in blocks:
                q = q_ref[hh, rows(qi), :]
                first = qi - (n - 1)
                acc, carry = window(q, k_ref[hh, rows(first, n), :], v_ref[hh, rows(first, n), :],
                                    n, True, None)
                state.append((hh, qi, first, q, acc, carry))
        for hh, qi, first, q, acc, carry in state:
            if not isinstance(qi, int):
                def cond(st):
                    return jnp.logical_and(st[0] >= 0, jnp.min(st[1]) < UNDERFLOW_BITS)

                def body(st, hh=hh, q=q):
                    kb, carry, acc = st
                    part, carry = window(q, k_ref[hh, rows(kb), :], v_ref[hh, rows(kb), :],
                                         1, False, carry)
                    return kb - 1, carry, acc + part

                _, _, acc = lax.while_loop(cond, body, (first - 1, carry, acc))
            o_ref[rows(qi), hh * HEAD_DIM:(hh + 1) * HEAD_DIM] = acc.astype(o_ref.dtype)

    for qi in range(min(lead - 1, nq)):
        run_blocks([(qi, qi + 1)])

    def group_body(g, _):
        base = lead - 1 + g * group
        run_blocks([(base + t, lead) for t in range(group)])
        return 0

    if nq > lead - 1:
        lax.fori_loop(0, (nq - (lead - 1)) // group, group_body, 0)


def _sb_attention(z_heads, n_heads, batch, seq):
    heads = 2 if n_heads % 2 == 0 else 1
    nq = seq // LANES
    lead = 3
    rest = max(nq - (lead - 1), 0)
    group = 2 if rest % 2 == 0 else 1
    hb = n_heads // heads
    blocks = 4 * _nbytes((heads, seq, HEAD_DIM), BF16)
    kern = functools.partial(_attn_kernel, heads=heads, nq=nq, group=group, lead=lead,
                             c2=HEAD_DIM ** -0.5 * LOG2E)
    return pl.pallas_call(
        kern,
        grid=(batch, hb),
        in_specs=[pl.BlockSpec((None, heads, seq, HEAD_DIM), lambda b, h: (b, h, 0, 0)),
                  pl.BlockSpec((None, heads, seq, HEAD_DIM), lambda b, h: (b, hb + h, 0, 0)),
                  pl.BlockSpec((None, heads, seq, HEAD_DIM), lambda b, h: (b, 2 * hb + h, 0, 0))],
        out_specs=pl.BlockSpec((None, seq, heads * HEAD_DIM), lambda b, h: (b, 0, h)),
        out_shape=jax.ShapeDtypeStruct((batch, seq, n_heads * HEAD_DIM), BF16),
        compiler_params=_params(("parallel", "parallel"), blocks, 8 * MIB),
        name="sb_attention",
    )(z_heads, z_heads, z_heads)


def _out_proj_kernel(yp_ref, ys_ref, yb_ref, w_ref, h_ref, g_ref, ho_ref, xn_ref, row_ref, *,
                     pw, sw, nj, tn):
    j = pl.program_id(1)
    acc = jnp.dot(yp_ref[...], w_ref[0:pw, :], preferred_element_type=F32)
    acc += jnp.dot(ys_ref[...], w_ref[pw:pw + sw, :], preferred_element_type=F32)
    acc += jnp.dot(yb_ref[...], w_ref[pw + sw:, :], preferred_element_type=F32)
    hn = h_ref[...] + acc
    ho_ref[...] = hn
    row_ref[j] = hn

    @pl.when(j == nj - 1)
    def _():
        ssq = None
        for jj in range(nj):
            t = row_ref[jj]
            part = jnp.sum(t * t, axis=-1, keepdims=True)
            ssq = part if ssq is None else ssq + part
        scale = lax.rsqrt(ssq * (1.0 / (nj * tn)) + RMS_EPS)
        for jj in range(nj):
            cols = slice(jj * tn, (jj + 1) * tn)
            xn_ref[:, cols] = (row_ref[jj] * scale * g_ref[:, cols]).astype(xn_ref.dtype)


def _out_proj(h, y_pool, y_sgu, y_sb, w_out, g_next):
    m, d = h.shape
    pw, sw, bw = y_pool.shape[1], y_sgu.shape[1], y_sb.shape[1]
    k = pw + sw + bw
    tm = _tile(m, 512, 8)
    tn = _tile(d, 512, LANES)
    nj = d // tn
    blocks = (_nbytes((tm, k), BF16) + _nbytes((k, tn), BF16) + 2 * _nbytes((tm, tn), F32)
              + _nbytes((1, d), F32) + _nbytes((tm, d), BF16))
    return pl.pallas_call(
        functools.partial(_out_proj_kernel, pw=pw, sw=sw, nj=nj, tn=tn),
        grid=(m // tm, nj),
        in_specs=[pl.BlockSpec((tm, pw), lambda i, j: (i, 0)),
                  pl.BlockSpec((tm, sw), lambda i, j: (i, 0)),
                  pl.BlockSpec((tm, bw), lambda i, j: (i, 0)),
                  pl.BlockSpec((k, tn), lambda i, j: (0, j)),
                  pl.BlockSpec((tm, tn), lambda i, j: (i, j)),
                  pl.BlockSpec((1, d), lambda i, j: (0, 0))],
        out_specs=[pl.BlockSpec((tm, tn), lambda i, j: (i, j)),
                   pl.BlockSpec((tm, d), lambda i, j: (i, 0))],
        out_shape=[jax.ShapeDtypeStruct((m, d), F32), jax.ShapeDtypeStruct((m, d), BF16)],
        scratch_shapes=[pltpu.VMEM((nj, tm, tn), F32)],
        compiler_params=_params(("parallel", "arbitrary"), blocks,
                                _nbytes((tm, d), F32) + 2 * _nbytes((tm, tn), F32)),
        name="out_proj",
    )(y_pool, y_sgu, y_sb, w_out, h, g_next)


def _gate_up_kernel(x_ref, wg_ref, wu_ref, o_ref):
    x = x_ref[...]
    g = jnp.dot(x, wg_ref[...], preferred_element_type=F32)
    u = jnp.dot(x, wu_ref[...], preferred_element_type=F32)
    o_ref[...] = (g * jax.nn.sigmoid(g) * u).astype(o_ref.dtype)


def _gate_up(xn, w_gate, w_up):
    m, k = xn.shape
    f = w_gate.shape[1]
    tm = _tile(m, 2048, 8)
    tf = _tile(f, 512, LANES)
    blocks = _nbytes((tm, k), BF16) + 2 * _nbytes((k, tf), BF16) + _nbytes((tm, tf), BF16)
    return pl.pallas_call(
        _gate_up_kernel,
        grid=(m // tm, f // tf),
        in_specs=[pl.BlockSpec((tm, k), lambda i, j: (i, 0)),
                  pl.BlockSpec((k, tf), lambda i, j: (0, j)),
                  pl.BlockSpec((k, tf), lambda i, j: (0, j))],
        out_specs=pl.BlockSpec((tm, tf), lambda i, j: (i, j)),
        out_shape=jax.ShapeDtypeStruct((m, f), BF16),
        compiler_params=_params(("parallel", "arbitrary"), blocks, 3 * _nbytes((tm, tf), F32)),
        name="ffn_gate_up",
    )(xn, w_gate, w_up)


def _down_kernel(a_ref, w_ref, h_ref, o_ref):
    o_ref[...] = h_ref[...] + jnp.dot(a_ref[...], w_ref[...], preferred_element_type=F32)


def _down_proj(h, act, w_down):
    m, d = h.shape
    f = act.shape[1]
    tm = _tile(m, 512, 8)
    tn = _tile(d, 256, LANES)
    blocks = _nbytes((tm, f), BF16) + _nbytes((f, tn), BF16) + 2 * _nbytes((tm, tn), F32)
    return pl.pallas_call(
        _down_kernel,
        grid=(m // tm, d // tn),
        in_specs=[pl.BlockSpec((tm, f), lambda i, j: (i, 0)),
                  pl.BlockSpec((f, tn), lambda i, j: (0, j)),
                  pl.BlockSpec((tm, tn), lambda i, j: (i, j))],
        out_specs=pl.BlockSpec((tm, tn), lambda i, j: (i, j)),
        out_shape=jax.ShapeDtypeStruct((m, d), F32),
        compiler_params=_params(("parallel", "arbitrary"), blocks, _nbytes((tm, tn), F32)),
        name="ffn_down",
    )(act, w_down, h)


def _ple_kernel(h_ref, p_ref, gn_ref, gd_ref, gu_ref, pp_ref, gnext_ref, *out_refs):
    h = h_ref[...]
    gn = (h * _rms_scale(h) * gn_ref[...]).astype(BF16)
    low = jnp.dot(gn, gd_ref[...], preferred_element_type=F32).astype(BF16)
    gate = jax.nn.sigmoid(jnp.dot(low, gu_ref[...], preferred_element_type=F32))
    emb = jnp.dot(p_ref[...].astype(BF16), pp_ref[...], preferred_element_type=F32)
    hn = h + gate * emb
    xn_ref = out_refs[-1]
    if len(out_refs) == 2:
        out_refs[0][...] = hn
    xn_ref[...] = (hn * _rms_scale(hn) * gnext_ref[...]).astype(xn_ref.dtype)


def _ple(h, p, g_ple, gate_down, gate_up, proj, g_next, last):
    m, d = h.shape
    e = p.shape[1]
    tm = _tile(m, 256, 8)
    row = pl.BlockSpec((tm, d), lambda i: (i, 0))
    vec = pl.BlockSpec((1, d), lambda i: (0, 0))
    if last:
        out_specs = [row]
        out_shape = [jax.ShapeDtypeStruct((m, d), F32)]
        out_bytes = _nbytes((tm, d), F32)
    else:
        out_specs = [row, row]
        out_shape = [jax.ShapeDtypeStruct((m, d), F32), jax.ShapeDtypeStruct((m, d), BF16)]
        out_bytes = _nbytes((tm, d), F32) + _nbytes((tm, d), BF16)
    blocks = (_nbytes((tm, d), F32) + _nbytes((tm, e), F32) + 2 * _nbytes((1, d), F32)
              + 3 * _nbytes((d, e), BF16) + out_bytes)
    return pl.pallas_call(
        _ple_kernel,
        grid=(m // tm,),
        in_specs=[row,
                  pl.BlockSpec((tm, e), lambda i: (i, 0)),
                  vec,
                  pl.BlockSpec((d, e), lambda i: (0, 0)),
                  pl.BlockSpec((e, d), lambda i: (0, 0)),
                  pl.BlockSpec((e, d), lambda i: (0, 0)),
                  vec],
        out_specs=out_specs,
        out_shape=out_shape,
        compiler_params=_params(("parallel",), blocks, 4 * _nbytes((tm, d), F32)),
        name="ple_gate",
    )(h, p, g_ple, gate_down, gate_up, proj, g_next)


def kernel(x, p, norm_mix_w, w_in, pool_w, pool_scale, sgu_norm_w, sgu_w, sgu_b, w_out, norm_ffn_w,
           w_gate, w_up, w_down, norm_ple_w, ple_gate_down, ple_gate_up, ple_proj, final_norm_w):
    batch, seq, d = x.shape
    depth = w_in.shape[0]
    m = batch * seq
    pool_width = pool_scale.shape[-1]
    sgu_width = sgu_norm_w.shape[-1]
    sb_width = w_out.shape[1] - pool_width - sgu_width
    n_heads = sb_width // HEAD_DIM
    rows_width = pool_width + 2 * sgu_width
    assert w_in.shape[2] == rows_width + 3 * sb_width and sb_width % HEAD_DIM == 0
    assert seq % LANES == 0 and sgu_w.shape[1] == sgu_width // HEAD_DIM
    assert pool_w.shape[1] == len(POOL_WINDOWS) and pool_w.shape[1] * pool_w.shape[2] == pool_width

    h = x.reshape(m, d)
    xn = _rmsnorm(h, norm_mix_w[0].reshape(1, d))
    out = None
    for i in range(depth):
        w_in_i = w_in[i].astype(BF16)
        z_rows = _in_proj_rows(xn, w_in_i, 0, rows_width)
        z_heads = _in_proj_heads(xn, w_in_i, rows_width, 3 * sb_width, batch, seq)
        y_pool = _pool_mixer(z_rows, pool_w[i].astype(BF16), pool_scale[i].reshape(1, pool_width), seq)
        y_sgu = _sgu_mixer(z_rows, pool_width, sgu_norm_w[i].reshape(1, sgu_width), sgu_w[i],
                           sgu_b[i].T)
        y_sb = _sb_attention(z_heads, n_heads, batch, seq).reshape(m, sb_width)
        h, xn = _out_proj(h, y_pool, y_sgu, y_sb, w_out[i].astype(BF16), norm_ffn_w[i].reshape(1, d))
        act = _gate_up(xn, w_gate[i].astype(BF16), w_up[i].astype(BF16))
        h = _down_proj(h, act, w_down[i].astype(BF16))
        last = i == depth - 1
        g_next = final_norm_w if last else norm_mix_w[i + 1]
        res = _ple(h, p[i].reshape(m, -1), norm_ple_w[i].reshape(1, d), ple_gate_down[i].astype(BF16),
                   ple_gate_up[i].astype(BF16), ple_proj[i].astype(BF16), g_next.reshape(1, d), last)
        if last:
            out = res[0]
        else:
            h, xn = res
    return out.reshape(batch, seq, d)
```

```python
import functools
import math

import jax
import jax.numpy as jnp
from jax import lax
from jax.experimental import pallas as pl
from jax.experimental.pallas import tpu as pltpu

F32 = jnp.float32
BF16 = jnp.bfloat16

RMS_EPS = 1e-6
LN_EPS = 1e-5
POOL_WINDOWS = (2, 4, 8, 16)
POOL_HALO = 16
HEAD_DIM = 128
LANES = 128
MIB = 1 << 20
V7X_VMEM_BYTES = 64 * MIB
VMEM_LIMIT_CAP = V7X_VMEM_BYTES - 4 * MIB
INTERNAL_SCRATCH_BYTES = 6 * MIB
LOG2E = math.log2(math.e)
UNDERFLOW_BITS = 150.0
OUT_OF_KEYS_CARRY = 1e30


def _tile(n, target, align):
    if n <= target:
        return n
    t = (target // align) * align
    while t >= align:
        if n % t == 0:
            return t
        t -= align
    raise ValueError(f"no tile for n={n} target={target} align={align}")


def _nbytes(shape, dtype):
    return math.prod(shape) * jnp.dtype(dtype).itemsize


def _params(semantics, pipelined_bytes, resident_bytes=0):
    need = 2 * pipelined_bytes + resident_bytes + INTERNAL_SCRATCH_BYTES
    return pltpu.CompilerParams(
        dimension_semantics=semantics,
        vmem_limit_bytes=int(min(VMEM_LIMIT_CAP, max(need, 16 * MIB))),
    )


def _rms_scale(x):
    return lax.rsqrt(jnp.mean(x * x, axis=-1, keepdims=True) + RMS_EPS)


def _rmsnorm_kernel(x_ref, g_ref, o_ref):
    x = x_ref[...]
    o_ref[...] = (x * _rms_scale(x) * g_ref[...]).astype(o_ref.dtype)


def _rmsnorm(x, g):
    m, d = x.shape
    tm = _tile(m, 512, 8)
    return pl.pallas_call(
        _rmsnorm_kernel,
        grid=(m // tm,),
        in_specs=[pl.BlockSpec((tm, d), lambda i: (i, 0)), pl.BlockSpec((1, d), lambda i: (0, 0))],
        out_specs=pl.BlockSpec((tm, d), lambda i: (i, 0)),
        out_shape=jax.ShapeDtypeStruct((m, d), BF16),
        compiler_params=_params(("parallel",), _nbytes((tm, d), F32) + _nbytes((tm, d), BF16),
                                _nbytes((tm, d), F32)),
        name="rmsnorm",
    )(x, g)


def _matmul_kernel(a_ref, w_ref, o_ref):
    o_ref[...] = jnp.dot(a_ref[...], w_ref[...], preferred_element_type=F32).astype(o_ref.dtype)


def _in_proj_rows(xn, w, layer, col0, ncols):
    m, k = xn.shape
    tm = _tile(m, 1024, 8)
    tn = _tile(math.gcd(ncols, col0) if col0 else ncols, 1024, LANES)
    off = col0 // tn
    blocks = _nbytes((tm, k), BF16) + _nbytes((k, tn), BF16) + _nbytes((tm, tn), BF16)
    return pl.pallas_call(
        _matmul_kernel,
        grid=(m // tm, ncols // tn),
        in_specs=[pl.BlockSpec((tm, k), lambda i, j: (i, 0)),
                  pl.BlockSpec((None, k, tn), lambda i, j: (layer, 0, j + off))],
        out_specs=pl.BlockSpec((tm, tn), lambda i, j: (i, j)),
        out_shape=jax.ShapeDtypeStruct((m, ncols), BF16),
        compiler_params=_params(("parallel", "arbitrary"), blocks, _nbytes((tm, tn), F32)),
        name="in_proj_rows",
    )(xn, w)


def _matmul_heads_kernel(a_ref, w_ref, o_ref, *, heads_per_tile):
    res = jnp.dot(a_ref[...], w_ref[...], preferred_element_type=F32)
    for hh in range(heads_per_tile):
        o_ref[hh] = res[:, hh * HEAD_DIM:(hh + 1) * HEAD_DIM].astype(o_ref.dtype)


def _in_proj_heads(xn, w, layer, col0, ncols, batch, seq):
    m, k = xn.shape
    tm = _tile(seq, 1024, 8)
    tn = _tile(math.gcd(ncols, col0), 1024, LANES)
    off = col0 // tn
    hpt = tn // HEAD_DIM
    spt = seq // tm
    blocks = _nbytes((tm, k), BF16) + _nbytes((k, tn), BF16) + _nbytes((tm, tn), BF16)
    return pl.pallas_call(
        functools.partial(_matmul_heads_kernel, heads_per_tile=hpt),
        grid=(m // tm, ncols // tn),
        in_specs=[pl.BlockSpec((tm, k), lambda i, j: (i, 0)),
                  pl.BlockSpec((None, k, tn), lambda i, j: (layer, 0, j + off))],
        out_specs=pl.BlockSpec((None, hpt, tm, HEAD_DIM), lambda i, j: (i // spt, j, i % spt, 0)),
        out_shape=jax.ShapeDtypeStruct((batch, ncols // HEAD_DIM, seq, HEAD_DIM), BF16),
        compiler_params=_params(("parallel", "arbitrary"), blocks, _nbytes((tm, tn), F32)),
        name="in_proj_heads",
    )(xn, w)


def _pool_kernel(a_ref, halo_ref, pw_ref, ps_ref, o_ref, ext_ref, *, seq, tm, group):
    t0 = lax.rem(pl.program_id(0) * tm, seq)
    a = a_ref[...].astype(F32)
    ext_ref[0:POOL_HALO, :] = jnp.where(t0 == 0, 0.0, halo_ref[...].astype(F32))
    ext_ref[POOL_HALO:, :] = a
    pos = t0 + lax.broadcasted_iota(jnp.int32, (tm, 1), 0)
    for g, w in enumerate(POOL_WINDOWS):
        cols = slice(g * group, (g + 1) * group)
        win = a[:, cols]
        for back in range(1, w):
            win = win + ext_ref[POOL_HALO - back:POOL_HALO - back + tm, cols]
        cnt = jnp.minimum(pos + 1, w).astype(F32)
        pooled = win / cnt - a[:, cols]
        y = jnp.dot(pooled.astype(BF16), pw_ref[g], preferred_element_type=F32)
        o_ref[:, cols] = (y * ps_ref[:, cols]).astype(o_ref.dtype)


def _pool_mixer(z_rows, pool_w, pool_scale, seq):
    m = z_rows.shape[0]
    n_groups, group, _ = pool_w.shape
    width = n_groups * group
    tm = _tile(seq, 512, POOL_HALO)
    hpt = tm // POOL_HALO
    blocks = (_nbytes((tm, width), BF16) * 2 + _nbytes((POOL_HALO, width), BF16)
              + _nbytes(pool_w.shape, BF16) + _nbytes((1, width), F32))
    return pl.pallas_call(
        functools.partial(_pool_kernel, seq=seq, tm=tm, group=group),
        grid=(m // tm,),
        in_specs=[pl.BlockSpec((tm, width), lambda i: (i, 0)),
                  pl.BlockSpec((POOL_HALO, width), lambda i: (jnp.maximum(i * hpt - 1, 0), 0)),
                  pl.BlockSpec(pool_w.shape, lambda i: (0, 0, 0)),
                  pl.BlockSpec((1, width), lambda i: (0, 0))],
        out_specs=pl.BlockSpec((tm, width), lambda i: (i, 0)),
        out_shape=jax.ShapeDtypeStruct((m, width), BF16),
        scratch_shapes=[pltpu.VMEM((tm + POOL_HALO, width), F32)],
        compiler_params=_params(("parallel",), blocks, 4 * _nbytes((tm + POOL_HALO, width), F32)),
        name="pool_mixer",
    )(z_rows, z_rows, pool_w, pool_scale)


def _gelu_tanh(x):
    return x * (0.5 * (1.0 + jnp.tanh(math.sqrt(2.0 / math.pi) * (x + 0.044715 * (x * x * x)))))


def _sgu_kernel(u_ref, v_ref, nw_ref, ws_ref, b_ref, o_ref, *, heads, chunks):
    v = _gelu_tanh(v_ref[...].astype(F32))
    vc = v - jnp.mean(v, axis=-1, keepdims=True)
    var = jnp.mean(vc * vc, axis=-1, keepdims=True)
    vn = (vc * lax.rsqrt(var + LN_EPS) * nw_ref[...]).astype(BF16)
    row = lax.broadcasted_iota(jnp.int32, (HEAD_DIM, HEAD_DIM), 0)
    col = lax.broadcasted_iota(jnp.int32, (HEAD_DIM, HEAD_DIM), 1)
    causal = row >= col
    for h in range(heads):
        cols = slice(h * HEAD_DIM, (h + 1) * HEAD_DIM)
        w = jnp.where(causal, ws_ref[h], 0.0).astype(BF16)
        bias = b_ref[:, h:h + 1]
        for c in range(chunks):
            rows = slice(c * HEAD_DIM, (c + 1) * HEAD_DIM)
            mixed = jnp.dot(w, vn[rows, cols], preferred_element_type=F32) + bias
            u = _gelu_tanh(u_ref[rows, cols].astype(F32))
            o_ref[rows, cols] = (u * mixed).astype(o_ref.dtype)


def _sgu_mixer(z_rows, col0, norm_w, w_s, b_t):
    m = z_rows.shape[0]
    heads = w_s.shape[0]
    width = heads * HEAD_DIM
    assert col0 % width == 0 and w_s.shape[1:] == (HEAD_DIM, HEAD_DIM)
    cb = col0 // width
    tm = _tile(m, 256, HEAD_DIM)
    blocks = (3 * _nbytes((tm, width), BF16) + _nbytes((1, width), F32) + _nbytes(w_s.shape, F32)
              + _nbytes((HEAD_DIM, LANES), F32))
    return pl.pallas_call(
        functools.partial(_sgu_kernel, heads=heads, chunks=tm // HEAD_DIM),
        grid=(m // tm,),
        in_specs=[pl.BlockSpec((tm, width), lambda i: (i, cb)),
                  pl.BlockSpec((tm, width), lambda i: (i, cb + 1)),
                  pl.BlockSpec((1, width), lambda i: (0, 0)),
                  pl.BlockSpec(w_s.shape, lambda i: (0, 0, 0)),
                  pl.BlockSpec(b_t.shape, lambda i: (0, 0))],
        out_specs=pl.BlockSpec((tm, width), lambda i: (i, 0)),
        out_shape=jax.ShapeDtypeStruct((m, width), BF16),
        compiler_params=_params(("parallel",), blocks, 4 * _nbytes((tm, width), F32)),
        name="sgu_mixer",
    )(z_rows, z_rows, norm_w, w_s, b_t)


def _neg_abs(x):
    bits = lax.bitcast_convert_type(x, jnp.uint32) | jnp.uint32(0x80000000)
    return lax.bitcast_convert_type(bits, F32)


def _bf16_split(x):
    bits = lax.bitcast_convert_type(x, jnp.uint32) & jnp.uint32(0xFFFF0000)
    hi = lax.bitcast_convert_type(bits, F32)
    return hi, x - hi


def _sb_windows(streams, n, diag, strict, cumsum_rhs, c2):
    scores = [lax.dot_general(q, kwin, (((1,), (1,)), ((), ())), preferred_element_type=F32)
              for q, kwin, _, _ in streams]
    log_betas, sums = [], []
    for s in scores:
        z2 = s * c2
        sp = jnp.maximum(z2, 0.0) + jnp.log(1.0 + jnp.exp2(_neg_abs(z2))) * LOG2E
        log_betas.append(z2 - sp)
        lhs = []
        for j in range(n):
            spj = sp[:, j * LANES:(j + 1) * LANES]
            if diag and j == n - 1:
                spj = jnp.where(strict, spj, 0.0)
            hi, lo = _bf16_split(spj)
            lhs.append(jnp.concatenate([hi.astype(BF16), lo.astype(BF16)], axis=1))
        lhs = lhs[0] if n == 1 else jnp.concatenate(lhs, axis=0)
        sums.append(jnp.dot(lhs, cumsum_rhs, preferred_element_type=F32))
    out = []
    for (_, _, vwin, carry), log_beta, both in zip(streams, log_betas, sums):
        weights = [None] * n
        for j in reversed(range(n)):
            inner = both[j * LANES:(j + 1) * LANES, :LANES]
            total = both[j * LANES:(j + 1) * LANES, LANES:]
            suffix = inner if carry is None else inner + carry
            pj = jnp.exp2(log_beta[:, j * LANES:(j + 1) * LANES] - suffix)
            if diag and j == n - 1:
                pj = jnp.where(strict, pj, 0.0)
            weights[j] = pj.astype(BF16)
            carry = total if carry is None else carry + total
        p = weights[0] if n == 1 else jnp.concatenate(weights, axis=1)
        out.append((jnp.dot(p, vwin, preferred_element_type=F32), carry))
    return out


def _attn_kernel(q_ref, k_ref, v_ref, o_ref, *, heads, nq, group, lead, c2):
    row = lax.broadcasted_iota(jnp.int32, (LANES, LANES), 0)
    col = lax.broadcasted_iota(jnp.int32, (LANES, LANES), 1)
    strict = row > col
    half = jnp.concatenate([strict.astype(BF16), jnp.ones((LANES, LANES), BF16)], axis=1)
    cumsum_rhs = jnp.concatenate([half, half], axis=0)
    windows = functools.partial(_sb_windows, strict=strict, cumsum_rhs=cumsum_rhs, c2=c2)

    def rows(blk, n=1):
        if isinstance(blk, int):
            return slice(blk * LANES, (blk + n) * LANES)
        return pl.ds(pl.multiple_of(blk * LANES, LANES), n * LANES)

    def run_blocks(qis, n):
        tasks = [(hh, qi) for hh in range(heads) for qi in qis]
        streams = [(q_ref[hh, rows(qi), :], k_ref[hh, rows(qi - (n - 1), n), :],
                    v_ref[hh, rows(qi - (n - 1), n), :], None) for hh, qi in tasks]
        done = windows(streams, n, True)
        accs = tuple(acc for acc, _ in done)
        if not isinstance(qis[0], int):
            def out_of_keys(qi, step):
                return jnp.where(qi - n - step < 0, OUT_OF_KEYS_CARRY, 0.0)

            def cond(st):
                step, carries, _ = st
                least = None
                for (_, qi), carry in zip(tasks, carries):
                    carry = carry + out_of_keys(qi, step)
                    least = carry if least is None else jnp.minimum(least, carry)
                return jnp.min(least) < UNDERFLOW_BITS

            def body(st):
                step, carries, accs = st
                more = []
                for (hh, qi), carry in zip(tasks, carries):
                    kb = jnp.maximum(qi - n - step, 0)
                    more.append((q_ref[hh, rows(qi), :], k_ref[hh, rows(kb), :],
                                 v_ref[hh, rows(kb), :], carry + out_of_keys(qi, step)))
                parts = windows(more, 1, False)
                return (step + 1, tuple(carry for _, carry in parts),
                        tuple(acc + part for acc, (part, _) in zip(accs, parts)))

            _, _, accs = lax.while_loop(cond, body, (0, tuple(carry for _, carry in done), accs))
        for (hh, qi), acc in zip(tasks, accs):
            o_ref[rows(qi), hh * HEAD_DIM:(hh + 1) * HEAD_DIM] = acc.astype(o_ref.dtype)

    for qi in range(min(lead - 1, nq)):
        run_blocks([qi], qi + 1)

    def group_body(g, _):
        base = lead - 1 + g * group
        run_blocks([base + t for t in range(group)], lead)
        return 0

    if nq > lead - 1:
        lax.fori_loop(0, (nq - (lead - 1)) // group, group_body, 0)


def _sb_attention(z_heads, n_heads, batch, seq):
    heads = math.gcd(n_heads, 4)
    nq = seq // LANES
    lead = 3
    rest = max(nq - (lead - 1), 0)
    group = 2 if rest % 2 == 0 else 1
    hb = n_heads // heads
    blocks = 4 * _nbytes((heads, seq, HEAD_DIM), BF16)
    kern = functools.partial(_attn_kernel, heads=heads, nq=nq, group=group, lead=lead,
                             c2=HEAD_DIM ** -0.5 * LOG2E)
    return pl.pallas_call(
        kern,
        grid=(batch, hb),
        in_specs=[pl.BlockSpec((None, heads, seq, HEAD_DIM), lambda b, h: (b, h, 0, 0)),
                  pl.BlockSpec((None, heads, seq, HEAD_DIM), lambda b, h: (b, hb + h, 0, 0)),
                  pl.BlockSpec((None, heads, seq, HEAD_DIM), lambda b, h: (b, 2 * hb + h, 0, 0))],
        out_specs=pl.BlockSpec((None, seq, heads * HEAD_DIM), lambda b, h: (b, 0, h)),
        out_shape=jax.ShapeDtypeStruct((batch, seq, n_heads * HEAD_DIM), BF16),
        compiler_params=_params(("parallel", "parallel"), blocks, 8 * MIB),
        name="sb_attention",
    )(z_heads, z_heads, z_heads)


def _out_proj_kernel(yp_ref, ys_ref, yb_ref, w_ref, h_ref, g_ref, ho_ref, hg_ref, rs_ref,
                     ssq_ref, *, pw, sw, nj, d):
    j = pl.program_id(1)

    @pl.when(j == 0)
    def _():
        ssq_ref[...] = jnp.zeros_like(ssq_ref)

    acc = jnp.dot(yp_ref[...], w_ref[0:pw, :], preferred_element_type=F32)
    acc += jnp.dot(ys_ref[...], w_ref[pw:pw + sw, :], preferred_element_type=F32)
    acc += jnp.dot(yb_ref[...], w_ref[pw + sw:, :], preferred_element_type=F32)
    hn = h_ref[...] + acc
    ho_ref[...] = hn
    hg_ref[...] = (hn * g_ref[...]).astype(hg_ref.dtype)
    ssq_ref[...] += jnp.sum(hn * hn, axis=-1, keepdims=True)

    @pl.when(j == nj - 1)
    def _():
        rs_ref[...] = lax.rsqrt(ssq_ref[...] * (1.0 / d) + RMS_EPS)


def _out_proj(h, y_pool, y_sgu, y_sb, w_out, layer, g_next):
    m, d = h.shape
    pw, sw, bw = y_pool.shape[1], y_sgu.shape[1], y_sb.shape[1]
    k = pw + sw + bw
    tm = _tile(m, 1024, 8)
    tn = _tile(d, 512, LANES)
    nj = d // tn
    blocks = (_nbytes((tm, k), BF16) + _nbytes((k, tn), BF16) + 2 * _nbytes((tm, tn), F32)
              + _nbytes((8, tn), F32) + _nbytes((tm, tn), BF16) + _nbytes((tm, LANES), F32))
    return pl.pallas_call(
        functools.partial(_out_proj_kernel, pw=pw, sw=sw, nj=nj, d=d),
        grid=(m // tm, nj),
        in_specs=[pl.BlockSpec((tm, pw), lambda i, j: (i, 0)),
                  pl.BlockSpec((tm, sw), lambda i, j: (i, 0)),
                  pl.BlockSpec((tm, bw), lambda i, j: (i, 0)),
                  pl.BlockSpec((None, k, tn), lambda i, j: (layer, 0, j)),
                  pl.BlockSpec((tm, tn), lambda i, j: (i, j)),
                  pl.BlockSpec((1, tn), lambda i, j: (0, j))],
        out_specs=[pl.BlockSpec((tm, tn), lambda i, j: (i, j)),
                   pl.BlockSpec((tm, tn), lambda i, j: (i, j)),
                   pl.BlockSpec((tm, 1), lambda i, j: (i, 0))],
        out_shape=[jax.ShapeDtypeStruct((m, d), F32), jax.ShapeDtypeStruct((m, d), BF16),
                   jax.ShapeDtypeStruct((m, 1), F32)],
        scratch_shapes=[pltpu.VMEM((tm, 1), F32)],
        compiler_params=_params(("parallel", "arbitrary"), blocks,
                                _nbytes((tm, LANES), F32) + 3 * _nbytes((tm, tn), F32)),
        name="out_proj",
    )(y_pool, y_sgu, y_sb, w_out, h, g_next)


def _gate_up_kernel(x_ref, rs_ref, wg_ref, wu_ref, o_ref):
    x = x_ref[...]
    rs = rs_ref[...]
    g = jnp.dot(x, wg_ref[...], preferred_element_type=F32) * rs
    u = jnp.dot(x, wu_ref[...], preferred_element_type=F32) * rs
    o_ref[...] = (g * jax.nn.sigmoid(g) * u).astype(o_ref.dtype)


def _gate_up(hg, rs, w_gate, w_up, layer):
    m, k = hg.shape
    f = w_gate.shape[2]
    tm = _tile(m, 2048, 8)
    tf = _tile(f, 512, LANES)
    blocks = (_nbytes((tm, k), BF16) + _nbytes((tm, LANES), F32) + 2 * _nbytes((k, tf), BF16)
              + _nbytes((tm, tf), BF16))
    return pl.pallas_call(
        _gate_up_kernel,
        grid=(m // tm, f // tf),
        in_specs=[pl.BlockSpec((tm, k), lambda i, j: (i, 0)),
                  pl.BlockSpec((tm, 1), lambda i, j: (i, 0)),
                  pl.BlockSpec((None, k, tf), lambda i, j: (layer, 0, j)),
                  pl.BlockSpec((None, k, tf), lambda i, j: (layer, 0, j))],
        out_specs=pl.BlockSpec((tm, tf), lambda i, j: (i, j)),
        out_shape=jax.ShapeDtypeStruct((m, f), BF16),
        compiler_params=_params(("parallel", "arbitrary"), blocks, 3 * _nbytes((tm, tf), F32)),
        name="ffn_gate_up",
    )(hg, rs, w_gate, w_up)


def _down_kernel(a_ref, w_ref, h_ref, o_ref):
    o_ref[...] = h_ref[...] + jnp.dot(a_ref[...], w_ref[...], preferred_element_type=F32)


def _down_proj(h, act, w_down, layer):
    m, d = h.shape
    f = act.shape[1]
    tm = _tile(m, 512, 8)
    tn = _tile(d, 512, LANES)
    blocks = _nbytes((tm, f), BF16) + _nbytes((f, tn), BF16) + 2 * _nbytes((tm, tn), F32)
    return pl.pallas_call(
        _down_kernel,
        grid=(m // tm, d // tn),
        in_specs=[pl.BlockSpec((tm, f), lambda i, j: (i, 0)),
                  pl.BlockSpec((None, f, tn), lambda i, j: (layer, 0, j)),
                  pl.BlockSpec((tm, tn), lambda i, j: (i, j))],
        out_specs=pl.BlockSpec((tm, tn), lambda i, j: (i, j)),
        out_shape=jax.ShapeDtypeStruct((m, d), F32),
        compiler_params=_params(("parallel", "arbitrary"), blocks, _nbytes((tm, tn), F32)),
        name="ffn_down",
    )(act, w_down, h)


def _ple_kernel(h_ref, p_ref, gn_ref, gd_ref, gu_ref, pp_ref, gnext_ref, *out_refs):
    h = h_ref[...]
    gn = (h * _rms_scale(h) * gn_ref[...]).astype(BF16)
    low = jnp.dot(gn, gd_ref[...], preferred_element_type=F32).astype(BF16)
    gate = jax.nn.sigmoid(jnp.dot(low, gu_ref[...], preferred_element_type=F32))
    emb = jnp.dot(p_ref[...].astype(BF16), pp_ref[...], preferred_element_type=F32)
    hn = h + gate * emb
    xn_ref = out_refs[-1]
    if len(out_refs) == 2:
        out_refs[0][...] = hn
    xn_ref[...] = (hn * _rms_scale(hn) * gnext_ref[...]).astype(xn_ref.dtype)


def _ple(h, p, g_ple, gate_down, gate_up, proj, g_next, last):
    m, d = h.shape
    e = p.shape[1]
    tm = _tile(m, 256, 8)
    row = pl.BlockSpec((tm, d), lambda i: (i, 0))
    vec = pl.BlockSpec((1, d), lambda i: (0, 0))
    if last:
        out_specs = [row]
        out_shape = [jax.ShapeDtypeStruct((m, d), F32)]
        out_bytes = _nbytes((tm, d), F32)
    else:
        out_specs = [row, row]
        out_shape = [jax.ShapeDtypeStruct((m, d), F32), jax.ShapeDtypeStruct((m, d), BF16)]
        out_bytes = _nbytes((tm, d), F32) + _nbytes((tm, d), BF16)
    blocks = (_nbytes((tm, d), F32) + _nbytes((tm, e), F32) + 2 * _nbytes((1, d), F32)
              + 3 * _nbytes((d, e), BF16) + out_bytes)
    return pl.pallas_call(
        _ple_kernel,
        grid=(m // tm,),
        in_specs=[row,
                  pl.BlockSpec((tm, e), lambda i: (i, 0)),
                  vec,
                  pl.BlockSpec((d, e), lambda i: (0, 0)),
                  pl.BlockSpec((e, d), lambda i: (0, 0)),
                  pl.BlockSpec((e, d), lambda i: (0, 0)),
                  vec],
        out_specs=out_specs,
        out_shape=out_shape,
        compiler_params=_params(("parallel",), blocks, 4 * _nbytes((tm, d), F32)),
        name="ple_gate",
    )(h, p, g_ple, gate_down, gate_up, proj, g_next)


def kernel(x, p, norm_mix_w, w_in, pool_w, pool_scale, sgu_norm_w, sgu_w, sgu_b, w_out, norm_ffn_w,
           w_gate, w_up, w_down, norm_ple_w, ple_gate_down, ple_gate_up, ple_proj, final_norm_w):
    batch, seq, d = x.shape
    depth = w_in.shape[0]
    m = batch * seq
    pool_width = pool_scale.shape[-1]
    sgu_width = sgu_norm_w.shape[-1]
    sb_width = w_out.shape[1] - pool_width - sgu_width
    n_heads = sb_width // HEAD_DIM
    rows_width = pool_width + 2 * sgu_width
    assert w_in.shape[2] == rows_width + 3 * sb_width and sb_width % HEAD_DIM == 0
    assert seq % LANES == 0 and sgu_w.shape[1] == sgu_width // HEAD_DIM
    assert pool_w.shape[1] == len(POOL_WINDOWS) and pool_w.shape[1] * pool_w.shape[2] == pool_width

    w_in, w_out, w_gate, w_up, w_down = (w.astype(BF16) for w in (w_in, w_out, w_gate, w_up, w_down))
    h = x.reshape(m, d)
    xn = _rmsnorm(h, norm_mix_w[0].reshape(1, d))
    out = None
    for i in range(depth):
        z_rows = _in_proj_rows(xn, w_in, i, 0, rows_width)
        z_heads = _in_proj_heads(xn, w_in, i, rows_width, 3 * sb_width, batch, seq)
        y_pool = _pool_mixer(z_rows, pool_w[i].astype(BF16), pool_scale[i].reshape(1, pool_width), seq)
        y_sgu = _sgu_mixer(z_rows, pool_width, sgu_norm_w[i].reshape(1, sgu_width), sgu_w[i],
                           sgu_b[i].T)
        y_sb = _sb_attention(z_heads, n_heads, batch, seq).reshape(m, sb_width)
        h, hg, rs = _out_proj(h, y_pool, y_sgu, y_sb, w_out, i, norm_ffn_w[i].reshape(1, d))
        act = _gate_up(hg, rs, w_gate, w_up, i)
        h = _down_proj(h, act, w_down, i)
        last = i == depth - 1
        g_next = final_norm_w if last else norm_mix_w[i + 1]
        res = _ple(h, p[i].reshape(m, -1), norm_ple_w[i].reshape(1, d), ple_gate_down[i].astype(BF16),
                   ple_gate_up[i].astype(BF16), ple_proj[i].astype(BF16), g_next.reshape(1, d), last)
        if last:
            out = res[0]
        else:
            h, xn = res
    return out.reshape(batch, seq, d)
```

```python
import functools
import math

import jax
import jax.numpy as jnp
from jax import lax
from jax.experimental import pallas as pl
from jax.experimental.pallas import tpu as pltpu

F32 = jnp.float32
BF16 = jnp.bfloat16

RMS_EPS = 1e-6
LN_EPS = 1e-5
POOL_WINDOWS = (2, 4, 8, 16)
POOL_HALO = 16
HEAD_DIM = 128
LANES = 128
BF16_SUBLANES = 16
MIB = 1 << 20
V7X_VMEM_BYTES = 64 * MIB
VMEM_LIMIT_CAP = V7X_VMEM_BYTES - 4 * MIB
INTERNAL_SCRATCH_BYTES = 6 * MIB
LOG2E = math.log2(math.e)
UNDERFLOW_BITS = 150.0
OUT_OF_KEYS_CARRY = 1e30


def _tile(n, target, align):
    if n <= target:
        return n
    t = (target // align) * align
    while t >= align:
        if n % t == 0:
            return t
        t -= align
    raise ValueError(f"no tile for n={n} target={target} align={align}")


def _nbytes(shape, dtype):
    return math.prod(shape) * jnp.dtype(dtype).itemsize


def _params(semantics, pipelined_bytes, resident_bytes=0):
    need = 2 * pipelined_bytes + resident_bytes + INTERNAL_SCRATCH_BYTES
    return pltpu.CompilerParams(
        dimension_semantics=semantics,
        vmem_limit_bytes=int(min(VMEM_LIMIT_CAP, max(need, 16 * MIB))),
    )


def _rms_scale(x):
    return lax.rsqrt(jnp.mean(x * x, axis=-1, keepdims=True) + RMS_EPS)


def _rmsnorm_kernel(x_ref, g_ref, o_ref):
    x = x_ref[...]
    o_ref[...] = (x * _rms_scale(x) * g_ref[...]).astype(o_ref.dtype)


def _rmsnorm(x, g):
    m, d = x.shape
    tm = _tile(m, 512, 8)
    return pl.pallas_call(
        _rmsnorm_kernel,
        grid=(m // tm,),
        in_specs=[pl.BlockSpec((tm, d), lambda i: (i, 0)), pl.BlockSpec((1, d), lambda i: (0, 0))],
        out_specs=pl.BlockSpec((tm, d), lambda i: (i, 0)),
        out_shape=jax.ShapeDtypeStruct((m, d), BF16),
        compiler_params=_params(("parallel",), _nbytes((tm, d), F32) + _nbytes((tm, d), BF16),
                                _nbytes((tm, d), F32)),
        name="rmsnorm",
    )(x, g)


def _matmul_kernel(a_ref, w_ref, o_ref):
    o_ref[...] = jnp.dot(a_ref[...], w_ref[...], preferred_element_type=F32).astype(o_ref.dtype)


def _in_proj_rows(xn, w, col0, ncols):
    m, k = xn.shape
    tm = _tile(m, 1024, 8)
    tn = _tile(math.gcd(ncols, col0) if col0 else ncols, 1024, LANES)
    off = col0 // tn
    blocks = _nbytes((tm, k), BF16) + _nbytes((k, tn), BF16) + _nbytes((tm, tn), BF16)
    return pl.pallas_call(
        _matmul_kernel,
        grid=(m // tm, ncols // tn),
        in_specs=[pl.BlockSpec((tm, k), lambda i, j: (i, 0)),
                  pl.BlockSpec((k, tn), lambda i, j: (0, j + off))],
        out_specs=pl.BlockSpec((tm, tn), lambda i, j: (i, j)),
        out_shape=jax.ShapeDtypeStruct((m, ncols), BF16),
        compiler_params=_params(("parallel", "arbitrary"), blocks, _nbytes((tm, tn), F32)),
        name="in_proj_rows",
    )(xn, w)


def _matmul_heads_kernel(a_ref, w_ref, o_ref, *, heads_per_tile):
    res = jnp.dot(a_ref[...], w_ref[...], preferred_element_type=F32)
    for hh in range(heads_per_tile):
        o_ref[hh] = res[:, hh * HEAD_DIM:(hh + 1) * HEAD_DIM].astype(o_ref.dtype)


def _in_proj_heads(xn, w, col0, ncols, batch, seq):
    m, k = xn.shape
    tm = _tile(seq, 1024, 8)
    tn = _tile(math.gcd(ncols, col0), 1024, LANES)
    off = col0 // tn
    hpt = tn // HEAD_DIM
    spt = seq // tm
    blocks = _nbytes((tm, k), BF16) + _nbytes((k, tn), BF16) + _nbytes((tm, tn), BF16)
    return pl.pallas_call(
        functools.partial(_matmul_heads_kernel, heads_per_tile=hpt),
        grid=(m // tm, ncols // tn),
        in_specs=[pl.BlockSpec((tm, k), lambda i, j: (i, 0)),
                  pl.BlockSpec((k, tn), lambda i, j: (0, j + off))],
        out_specs=pl.BlockSpec((None, hpt, tm, HEAD_DIM), lambda i, j: (i // spt, j, i % spt, 0)),
        out_shape=jax.ShapeDtypeStruct((batch, ncols // HEAD_DIM, seq, HEAD_DIM), BF16),
        compiler_params=_params(("parallel", "arbitrary"), blocks, _nbytes((tm, tn), F32)),
        name="in_proj_heads",
    )(xn, w)


def _pool_kernel(a_ref, halo_ref, pw_ref, ps_ref, o_ref, ext_ref, *, seq, tm, group):
    t0 = lax.rem(pl.program_id(0) * tm, seq)
    a = a_ref[...].astype(F32)
    ext_ref[0:POOL_HALO, :] = jnp.where(t0 == 0, 0.0, halo_ref[...].astype(F32))
    ext_ref[POOL_HALO:, :] = a
    pos = t0 + lax.broadcasted_iota(jnp.int32, (tm, 1), 0)
    for g, w in enumerate(POOL_WINDOWS):
        cols = slice(g * group, (g + 1) * group)
        win = a[:, cols]
        for back in range(1, w):
            win = win + ext_ref[POOL_HALO - back:POOL_HALO - back + tm, cols]
        cnt = jnp.minimum(pos + 1, w).astype(F32)
        pooled = win / cnt - a[:, cols]
        y = jnp.dot(pooled.astype(BF16), pw_ref[g], preferred_element_type=F32)
        o_ref[:, cols] = (y * ps_ref[:, cols]).astype(o_ref.dtype)


def _pool_mixer(z_rows, pool_w, pool_scale, seq):
    m = z_rows.shape[0]
    n_groups, group, _ = pool_w.shape
    width = n_groups * group
    tm = _tile(seq, 512, POOL_HALO)
    hpt = tm // POOL_HALO
    blocks = (_nbytes((tm, width), BF16) * 2 + _nbytes((POOL_HALO, width), BF16)
              + _nbytes(pool_w.shape, BF16) + _nbytes((1, width), F32))
    return pl.pallas_call(
        functools.partial(_pool_kernel, seq=seq, tm=tm, group=group),
        grid=(m // tm,),
        in_specs=[pl.BlockSpec((tm, width), lambda i: (i, 0)),
                  pl.BlockSpec((POOL_HALO, width), lambda i: (jnp.maximum(i * hpt - 1, 0), 0)),
                  pl.BlockSpec(pool_w.shape, lambda i: (0, 0, 0)),
                  pl.BlockSpec((1, width), lambda i: (0, 0))],
        out_specs=pl.BlockSpec((tm, width), lambda i: (i, 0)),
        out_shape=jax.ShapeDtypeStruct((m, width), BF16),
        scratch_shapes=[pltpu.VMEM((tm + POOL_HALO, width), F32)],
        compiler_params=_params(("parallel",), blocks, 4 * _nbytes((tm + POOL_HALO, width), F32)),
        name="pool_mixer",
    )(z_rows, z_rows, pool_w, pool_scale)


def _gelu_tanh(x):
    return x * (0.5 * (1.0 + jnp.tanh(math.sqrt(2.0 / math.pi) * (x + 0.044715 * (x * x * x)))))


def _sgu_kernel(u_ref, v_ref, nw_ref, ws_ref, b_ref, o_ref, *, heads, chunks):
    v = _gelu_tanh(v_ref[...].astype(F32))
    vc = v - jnp.mean(v, axis=-1, keepdims=True)
    var = jnp.mean(vc * vc, axis=-1, keepdims=True)
    vn = (vc * lax.rsqrt(var + LN_EPS) * nw_ref[...]).astype(BF16)
    row = lax.broadcasted_iota(jnp.int32, (HEAD_DIM, HEAD_DIM), 0)
    col = lax.broadcasted_iota(jnp.int32, (HEAD_DIM, HEAD_DIM), 1)
    causal = row >= col
    for h in range(heads):
        cols = slice(h * HEAD_DIM, (h + 1) * HEAD_DIM)
        w = jnp.where(causal, ws_ref[h], 0.0).astype(BF16)
        bias = b_ref[:, h:h + 1]
        for c in range(chunks):
            rows = slice(c * HEAD_DIM, (c + 1) * HEAD_DIM)
            mixed = jnp.dot(w, vn[rows, cols], preferred_element_type=F32) + bias
            u = _gelu_tanh(u_ref[rows, cols].astype(F32))
            o_ref[rows, cols] = (u * mixed).astype(o_ref.dtype)


def _sgu_mixer(z_rows, col0, norm_w, w_s, b_t):
    m = z_rows.shape[0]
    heads = w_s.shape[0]
    width = heads * HEAD_DIM
    assert col0 % width == 0 and w_s.shape[1:] == (HEAD_DIM, HEAD_DIM)
    cb = col0 // width
    tm = _tile(m, 256, HEAD_DIM)
    blocks = (3 * _nbytes((tm, width), BF16) + _nbytes((1, width), F32) + _nbytes(w_s.shape, F32)
              + _nbytes((HEAD_DIM, LANES), F32))
    return pl.pallas_call(
        functools.partial(_sgu_kernel, heads=heads, chunks=tm // HEAD_DIM),
        grid=(m // tm,),
        in_specs=[pl.BlockSpec((tm, width), lambda i: (i, cb)),
                  pl.BlockSpec((tm, width), lambda i: (i, cb + 1)),
                  pl.BlockSpec((1, width), lambda i: (0, 0)),
                  pl.BlockSpec(w_s.shape, lambda i: (0, 0, 0)),
                  pl.BlockSpec(b_t.shape, lambda i: (0, 0))],
        out_specs=pl.BlockSpec((tm, width), lambda i: (i, 0)),
        out_shape=jax.ShapeDtypeStruct((m, width), BF16),
        compiler_params=_params(("parallel",), blocks, 4 * _nbytes((tm, width), F32)),
        name="sgu_mixer",
    )(z_rows, z_rows, norm_w, w_s, b_t)


def _neg_abs(x):
    bits = lax.bitcast_convert_type(x, jnp.uint32) | jnp.uint32(0x80000000)
    return lax.bitcast_convert_type(bits, F32)


def _bf16_split(x):
    bits = lax.bitcast_convert_type(x, jnp.uint32) & jnp.uint32(0xFFFF0000)
    hi = lax.bitcast_convert_type(bits, F32)
    return hi, x - hi


def _sb_windows(streams, n, diag, strict, cumsum_rhs, c2):
    scores = [lax.dot_general(q, kwin, (((1,), (1,)), ((), ())), preferred_element_type=F32)
              for q, kwin, _, _ in streams]
    log_betas, sums = [], []
    for s in scores:
        z2 = s * c2
        sp = jnp.maximum(z2, 0.0) + jnp.log(1.0 + jnp.exp2(_neg_abs(z2))) * LOG2E
        log_betas.append(z2 - sp)
        lhs = []
        for j in range(n):
            spj = sp[:, j * LANES:(j + 1) * LANES]
            if diag and j == n - 1:
                spj = jnp.where(strict, spj, 0.0)
            hi, lo = _bf16_split(spj)
            lhs.append(jnp.concatenate([hi.astype(BF16), lo.astype(BF16)], axis=1))
        lhs = lhs[0] if n == 1 else jnp.concatenate(lhs, axis=0)
        sums.append(jnp.dot(lhs, cumsum_rhs, preferred_element_type=F32))
    out = []
    for (_, _, vwin, carry), log_beta, both in zip(streams, log_betas, sums):
        weights = [None] * n
        for j in reversed(range(n)):
            inner = both[j * LANES:(j + 1) * LANES, :LANES]
            total = both[j * LANES:(j + 1) * LANES, LANES:]
            suffix = inner if carry is None else inner + carry
            pj = jnp.exp2(log_beta[:, j * LANES:(j + 1) * LANES] - suffix)
            if diag and j == n - 1:
                pj = jnp.where(strict, pj, 0.0)
            weights[j] = pj.astype(BF16)
            carry = total if carry is None else carry + total
        p = weights[0] if n == 1 else jnp.concatenate(weights, axis=1)
        out.append((jnp.dot(p, vwin, preferred_element_type=F32), carry))
    return out


def _attn_kernel(q_ref, k_ref, v_ref, o_ref, *, heads, nq, group, lead, c2):
    row = lax.broadcasted_iota(jnp.int32, (LANES, LANES), 0)
    col = lax.broadcasted_iota(jnp.int32, (LANES, LANES), 1)
    strict = row > col
    half = jnp.concatenate([strict.astype(BF16), jnp.ones((LANES, LANES), BF16)], axis=1)
    cumsum_rhs = jnp.concatenate([half, half], axis=0)
    windows = functools.partial(_sb_windows, strict=strict, cumsum_rhs=cumsum_rhs, c2=c2)

    def rows(blk, n=1):
        if isinstance(blk, int):
            return slice(blk * LANES, (blk + n) * LANES)
        return pl.ds(pl.multiple_of(blk * LANES, LANES), n * LANES)

    def run_blocks(qis, n):
        tasks = [(hh, qi) for hh in range(heads) for qi in qis]
        streams = [(q_ref[hh, rows(qi), :], k_ref[hh, rows(qi - (n - 1), n), :],
                    v_ref[hh, rows(qi - (n - 1), n), :], None) for hh, qi in tasks]
        done = windows(streams, n, True)
        accs = tuple(acc for acc, _ in done)
        if not isinstance(qis[0], int):
            def out_of_keys(qi, step):
                return jnp.where(qi - n - step < 0, OUT_OF_KEYS_CARRY, 0.0)

            def cond(st):
                step, carries, _ = st
                least = None
                for (_, qi), carry in zip(tasks, carries):
                    carry = carry + out_of_keys(qi, step)
                    least = carry if least is None else jnp.minimum(least, carry)
                return jnp.min(least) < UNDERFLOW_BITS

            def body(st):
                step, carries, accs = st
                more = []
                for (hh, qi), carry in zip(tasks, carries):
                    kb = jnp.maximum(qi - n - step, 0)
                    more.append((q_ref[hh, rows(qi), :], k_ref[hh, rows(kb), :],
                                 v_ref[hh, rows(kb), :], carry + out_of_keys(qi, step)))
                parts = windows(more, 1, False)
                return (step + 1, tuple(carry for _, carry in parts),
                        tuple(acc + part for acc, (part, _) in zip(accs, parts)))

            _, _, accs = lax.while_loop(cond, body, (0, tuple(carry for _, carry in done), accs))
        for (hh, qi), acc in zip(tasks, accs):
            o_ref[rows(qi), hh * HEAD_DIM:(hh + 1) * HEAD_DIM] = acc.astype(o_ref.dtype)

    for qi in range(min(lead - 1, nq)):
        run_blocks([qi], qi + 1)

    def group_body(g, _):
        base = lead - 1 + g * group
        run_blocks([base + t for t in range(group)], lead)
        return 0

    if nq > lead - 1:
        lax.fori_loop(0, (nq - (lead - 1)) // group, group_body, 0)


def _sb_attention(z_heads, n_heads, batch, seq):
    heads = math.gcd(n_heads, 4)
    nq = seq // LANES
    lead = 3
    rest = max(nq - (lead - 1), 0)
    group = 2 if rest % 2 == 0 else 1
    hb = n_heads // heads
    blocks = 4 * _nbytes((heads, seq, HEAD_DIM), BF16)
    kern = functools.partial(_attn_kernel, heads=heads, nq=nq, group=group, lead=lead,
                             c2=HEAD_DIM ** -0.5 * LOG2E)
    return pl.pallas_call(
        kern,
        grid=(batch, hb),
        in_specs=[pl.BlockSpec((None, heads, seq, HEAD_DIM), lambda b, h: (b, h, 0, 0)),
                  pl.BlockSpec((None, heads, seq, HEAD_DIM), lambda b, h: (b, hb + h, 0, 0)),
                  pl.BlockSpec((None, heads, seq, HEAD_DIM), lambda b, h: (b, 2 * hb + h, 0, 0))],
        out_specs=pl.BlockSpec((None, seq, heads * HEAD_DIM), lambda b, h: (b, 0, h)),
        out_shape=jax.ShapeDtypeStruct((batch, seq, n_heads * HEAD_DIM), BF16),
        compiler_params=_params(("parallel", "parallel"), blocks, 8 * MIB),
        name="sb_attention",
    )(z_heads, z_heads, z_heads)


def _out_proj_kernel(yp_ref, ys_ref, yb_ref, w_ref, h_ref, g_ref, ho_ref, hg_ref, rs_ref,
                     ssq_ref, *, pw, sw, nj, d):
    j = pl.program_id(1)

    @pl.when(j == 0)
    def _():
        ssq_ref[...] = jnp.zeros_like(ssq_ref)

    acc = jnp.dot(yp_ref[...], w_ref[0:pw, :], preferred_element_type=F32)
    acc += jnp.dot(ys_ref[...], w_ref[pw:pw + sw, :], preferred_element_type=F32)
    acc += jnp.dot(yb_ref[...], w_ref[pw + sw:, :], preferred_element_type=F32)
    hn = h_ref[...] + acc
    ho_ref[...] = hn
    hg_ref[...] = (hn * g_ref[...]).astype(hg_ref.dtype)
    ssq_ref[...] += jnp.sum(hn * hn, axis=-1, keepdims=True)

    @pl.when(j == nj - 1)
    def _():
        rs_ref[...] = lax.rsqrt(ssq_ref[...] * (1.0 / d) + RMS_EPS)


def _out_proj(h, y_pool, y_sgu, y_sb, w_out, g_next):
    m, d = h.shape
    pw, sw, bw = y_pool.shape[1], y_sgu.shape[1], y_sb.shape[1]
    k = pw + sw + bw
    tm = _tile(m, 1024, 8)
    tn = _tile(d, 512, LANES)
    nj = d // tn
    blocks = (_nbytes((tm, k), BF16) + _nbytes((k, tn), BF16) + 2 * _nbytes((tm, tn), F32)
              + _nbytes((8, tn), F32) + _nbytes((tm, tn), BF16) + _nbytes((tm, LANES), F32))
    return pl.pallas_call(
        functools.partial(_out_proj_kernel, pw=pw, sw=sw, nj=nj, d=d),
        grid=(m // tm, nj),
        in_specs=[pl.BlockSpec((tm, pw), lambda i, j: (i, 0)),
                  pl.BlockSpec((tm, sw), lambda i, j: (i, 0)),
                  pl.BlockSpec((tm, bw), lambda i, j: (i, 0)),
                  pl.BlockSpec((k, tn), lambda i, j: (0, j)),
                  pl.BlockSpec((tm, tn), lambda i, j: (i, j)),
                  pl.BlockSpec((1, tn), lambda i, j: (0, j))],
        out_specs=[pl.BlockSpec((tm, tn), lambda i, j: (i, j)),
                   pl.BlockSpec((tm, tn), lambda i, j: (i, j)),
                   pl.BlockSpec((tm, 1), lambda i, j: (i, 0))],
        out_shape=[jax.ShapeDtypeStruct((m, d), F32), jax.ShapeDtypeStruct((m, d), BF16),
                   jax.ShapeDtypeStruct((m, 1), F32)],
        scratch_shapes=[pltpu.VMEM((tm, 1), F32)],
        compiler_params=_params(("parallel", "arbitrary"), blocks,
                                _nbytes((tm, LANES), F32) + 3 * _nbytes((tm, tn), F32)),
        name="out_proj",
    )(y_pool, y_sgu, y_sb, w_out, h, g_next)


class _CastJobs:
    def __init__(self, jobs, grid):
        self.jobs = jobs
        self.plans = [self.plan(w.shape[1:], grid) for w, _ in jobs]
        assert all(p is not None for p in self.plans)

    @staticmethod
    def plan(shape, grid):
        rows, cols = shape
        for rows_on_first in (True, False):
            gr, gc = grid if rows_on_first else grid[::-1]
            if rows % gr == 0 and cols % gc == 0:
                block = (rows // gr, cols // gc)
                if block[0] % BF16_SUBLANES == 0 and block[1] % LANES == 0:
                    return block, rows_on_first
        return None

    def __len__(self):
        return len(self.jobs)

    def arrays(self):
        return [w for w, _ in self.jobs]

    def in_specs(self):
        return [pl.BlockSpec((None,) + block,
                             (lambda i, j, layer=layer: (layer, i, j)) if first else
                             (lambda i, j, layer=layer: (layer, j, i)))
                for (_, layer), (block, first) in zip(self.jobs, self.plans)]

    def out_specs(self):
        return [pl.BlockSpec(block, (lambda i, j: (i, j)) if first else (lambda i, j: (j, i)))
                for block, first in self.plans]

    def out_shapes(self):
        return [jax.ShapeDtypeStruct(w.shape[1:], BF16) for w, _ in self.jobs]

    def block_bytes(self):
        return sum(_nbytes(block, F32) + _nbytes(block, BF16) for block, _ in self.plans)


def _cast_kernel(w_ref, o_ref):
    o_ref[...] = w_ref[...].astype(o_ref.dtype)


def _cast_layer(w, layer):
    _, rows, cols = w.shape
    tr = _tile(rows, 128, BF16_SUBLANES)
    return pl.pallas_call(
        _cast_kernel,
        grid=(rows // tr,),
        in_specs=[pl.BlockSpec((None, tr, cols), lambda i: (layer, i, 0))],
        out_specs=pl.BlockSpec((tr, cols), lambda i: (i, 0)),
        out_shape=jax.ShapeDtypeStruct((rows, cols), BF16),
        compiler_params=_params(("parallel",), _nbytes((tr, cols), F32) + _nbytes((tr, cols), BF16)),
        name="cast_layer",
    )(w)


def _run_casts(refs, n):
    for src, dst in zip(refs[:n], refs[len(refs) - n:]):
        dst[...] = src[...].astype(dst.dtype)


def _gate_up_kernel(x_ref, rs_ref, wg_ref, wu_ref, *refs, chunks, n_casts):
    o_ref = refs[n_casts]
    rows_per_chunk = x_ref.shape[0] // chunks
    for c in range(chunks):
        rows = slice(c * rows_per_chunk, (c + 1) * rows_per_chunk)
        x = x_ref[rows, :]
        rs = rs_ref[rows, :]
        g = jnp.dot(x, wg_ref[...], preferred_element_type=F32) * rs
        u = jnp.dot(x, wu_ref[...], preferred_element_type=F32) * rs
        o_ref[rows, :] = (g * jax.nn.sigmoid(g) * u).astype(o_ref.dtype)
    _run_casts(refs, n_casts)


def _gate_up_grid(m, f):
    tm = _tile(m, 2048, 8)
    tf = _tile(f, 512, LANES)
    return tm, tf, (m // tm, f // tf)


def _gate_up(hg, rs, w_gate, w_up, cast_jobs):
    m, k = hg.shape
    f = w_gate.shape[1]
    tm, tf, grid = _gate_up_grid(m, f)
    casts = _CastJobs(cast_jobs, grid)
    blocks = (_nbytes((tm, k), BF16) + _nbytes((tm, LANES), F32) + 2 * _nbytes((k, tf), BF16)
              + _nbytes((tm, tf), BF16) + casts.block_bytes())
    return pl.pallas_call(
        functools.partial(_gate_up_kernel, chunks=2 if tm % 1024 == 0 else 1, n_casts=len(casts)),
        grid=grid,
        in_specs=[pl.BlockSpec((tm, k), lambda i, j: (i, 0)),
                  pl.BlockSpec((tm, 1), lambda i, j: (i, 0)),
                  pl.BlockSpec((k, tf), lambda i, j: (0, j)),
                  pl.BlockSpec((k, tf), lambda i, j: (0, j))] + casts.in_specs(),
        out_specs=[pl.BlockSpec((tm, tf), lambda i, j: (i, j))] + casts.out_specs(),
        out_shape=[jax.ShapeDtypeStruct((m, f), BF16)] + casts.out_shapes(),
        compiler_params=_params(("parallel", "arbitrary"), blocks, 3 * _nbytes((tm, tf), F32)),
        name="ffn_gate_up",
    )(hg, rs, w_gate, w_up, *casts.arrays())


def _down_kernel(a_ref, w_ref, h_ref, *refs, n_casts):
    o_ref = refs[n_casts]
    o_ref[...] = h_ref[...] + jnp.dot(a_ref[...], w_ref[...], preferred_element_type=F32)
    _run_casts(refs, n_casts)


def _down_grid(m, d):
    tm = _tile(m, 512, 8)
    tn = _tile(d, 512, LANES)
    return tm, tn, (m // tm, d // tn)


def _down_proj(h, act, w_down, cast_jobs):
    m, d = h.shape
    f = act.shape[1]
    tm, tn, grid = _down_grid(m, d)
    casts = _CastJobs(cast_jobs, grid)
    blocks = (_nbytes((tm, f), BF16) + _nbytes((f, tn), BF16) + 2 * _nbytes((tm, tn), F32)
              + casts.block_bytes())
    return pl.pallas_call(
        functools.partial(_down_kernel, n_casts=len(casts)),
        grid=grid,
        in_specs=[pl.BlockSpec((tm, f), lambda i, j: (i, 0)),
                  pl.BlockSpec((f, tn), lambda i, j: (0, j)),
                  pl.BlockSpec((tm, tn), lambda i, j: (i, j))] + casts.in_specs(),
        out_specs=[pl.BlockSpec((tm, tn), lambda i, j: (i, j))] + casts.out_specs(),
        out_shape=[jax.ShapeDtypeStruct((m, d), F32)] + casts.out_shapes(),
        compiler_params=_params(("parallel", "arbitrary"), blocks, _nbytes((tm, tn), F32)),
        name="ffn_down",
    )(act, w_down, h, *casts.arrays())


def _ple_kernel(h_ref, p_ref, gn_ref, gd_ref, gu_ref, pp_ref, gnext_ref, *out_refs):
    h = h_ref[...]
    gn = (h * _rms_scale(h) * gn_ref[...]).astype(BF16)
    low = jnp.dot(gn, gd_ref[...], preferred_element_type=F32).astype(BF16)
    gate = jax.nn.sigmoid(jnp.dot(low, gu_ref[...], preferred_element_type=F32))
    emb = jnp.dot(p_ref[...].astype(BF16), pp_ref[...], preferred_element_type=F32)
    hn = h + gate * emb
    xn_ref = out_refs[-1]
    if len(out_refs) == 2:
        out_refs[0][...] = hn
    xn_ref[...] = (hn * _rms_scale(hn) * gnext_ref[...]).astype(xn_ref.dtype)


def _ple(h, p, g_ple, gate_down, gate_up, proj, g_next, last):
    m, d = h.shape
    e = p.shape[1]
    tm = _tile(m, 256, 8)
    row = pl.BlockSpec((tm, d), lambda i: (i, 0))
    vec = pl.BlockSpec((1, d), lambda i: (0, 0))
    if last:
        out_specs = [row]
        out_shape = [jax.ShapeDtypeStruct((m, d), F32)]
        out_bytes = _nbytes((tm, d), F32)
    else:
        out_specs = [row, row]
        out_shape = [jax.ShapeDtypeStruct((m, d), F32), jax.ShapeDtypeStruct((m, d), BF16)]
        out_bytes = _nbytes((tm, d), F32) + _nbytes((tm, d), BF16)
    blocks = (_nbytes((tm, d), F32) + _nbytes((tm, e), F32) + 2 * _nbytes((1, d), F32)
              + 3 * _nbytes((d, e), BF16) + out_bytes)
    return pl.pallas_call(
        _ple_kernel,
        grid=(m // tm,),
        in_specs=[row,
                  pl.BlockSpec((tm, e), lambda i: (i, 0)),
                  vec,
                  pl.BlockSpec((d, e), lambda i: (0, 0)),
                  pl.BlockSpec((e, d), lambda i: (0, 0)),
                  pl.BlockSpec((e, d), lambda i: (0, 0)),
                  vec],
        out_specs=out_specs,
        out_shape=out_shape,
        compiler_params=_params(("parallel",), blocks, 4 * _nbytes((tm, d), F32)),
        name="ple_gate",
    )(h, p, g_ple, gate_down, gate_up, proj, g_next)


def kernel(x, p, norm_mix_w, w_in, pool_w, pool_scale, sgu_norm_w, sgu_w, sgu_b, w_out, norm_ffn_w,
           w_gate, w_up, w_down, norm_ple_w, ple_gate_down, ple_gate_up, ple_proj, final_norm_w):
    batch, seq, d = x.shape
    depth = w_in.shape[0]
    m = batch * seq
    pool_width = pool_scale.shape[-1]
    sgu_width = sgu_norm_w.shape[-1]
    sb_width = w_out.shape[1] - pool_width - sgu_width
    n_heads = sb_width // HEAD_DIM
    rows_width = pool_width + 2 * sgu_width
    assert w_in.shape[2] == rows_width + 3 * sb_width and sb_width % HEAD_DIM == 0
    assert seq % LANES == 0 and sgu_w.shape[1] == sgu_width // HEAD_DIM
    assert pool_w.shape[1] == len(POOL_WINDOWS) and pool_w.shape[1] * pool_w.shape[2] == pool_width

    f = w_gate.shape[2]
    gate_up_grid, down_grid = _gate_up_grid(m, f)[2], _down_grid(m, d)[2]
    ride_gate_up = all(_CastJobs.plan(w.shape[1:], gate_up_grid) for w in (w_down, w_gate, w_up))
    ride_down = all(_CastJobs.plan(w.shape[1:], down_grid) for w in (w_in, w_out))

    w_in_b, w_out_b, w_gate_b, w_up_b = (_cast_layer(w, 0) for w in (w_in, w_out, w_gate, w_up))
    h = x.reshape(m, d)
    xn = _rmsnorm(h, norm_mix_w[0].reshape(1, d))
    out = None
    for i in range(depth):
        more = i + 1 < depth
        z_rows = _in_proj_rows(xn, w_in_b, 0, rows_width)
        z_heads = _in_proj_heads(xn, w_in_b, rows_width, 3 * sb_width, batch, seq)
        y_pool = _pool_mixer(z_rows, pool_w[i].astype(BF16), pool_scale[i].reshape(1, pool_width), seq)
        y_sgu = _sgu_mixer(z_rows, pool_width, sgu_norm_w[i].reshape(1, sgu_width), sgu_w[i],
                           sgu_b[i].T)
        y_sb = _sb_attention(z_heads, n_heads, batch, seq).reshape(m, sb_width)
        h, hg, rs = _out_proj(h, y_pool, y_sgu, y_sb, w_out_b, norm_ffn_w[i].reshape(1, d))

        jobs = [(w_down, i)] + ([(w_gate, i + 1), (w_up, i + 1)] if more else [])
        act, *cast = _gate_up(hg, rs, w_gate_b, w_up_b, jobs if ride_gate_up else [])
        if ride_gate_up:
            w_down_b = cast[0]
            if more:
                w_gate_b, w_up_b = cast[1], cast[2]
        else:
            w_down_b = _cast_layer(w_down, i)
            if more:
                w_gate_b, w_up_b = _cast_layer(w_gate, i + 1), _cast_layer(w_up, i + 1)

        jobs = [(w_in, i + 1), (w_out, i + 1)] if more and ride_down else []
        h, *cast = _down_proj(h, act, w_down_b, jobs)
        if jobs:
            w_in_b, w_out_b = cast
        elif more:
            w_in_b, w_out_b = _cast_layer(w_in, i + 1), _cast_layer(w_out, i + 1)
        last = i == depth - 1
        g_next = final_norm_w if last else norm_mix_w[i + 1]
        res = _ple(h, p[i].reshape(m, -1), norm_ple_w[i].reshape(1, d), ple_gate_down[i].astype(BF16),
                   ple_gate_up[i].astype(BF16), ple_proj[i].astype(BF16), g_next.reshape(1, d), last)
        if last:
            out = res[0]
        else:
            h, xn = res
    return out.reshape(batch, seq, d)
```

```python
import functools
import math

import jax
import jax.numpy as jnp
from jax import lax
from jax.experimental import pallas as pl
from jax.experimental.pallas import tpu as pltpu

F32 = jnp.float32
BF16 = jnp.bfloat16

RMS_EPS = 1e-6
LN_EPS = 1e-5
POOL_WINDOWS = (2, 4, 8, 16)
POOL_HALO = 16
HEAD_DIM = 128
LANES = 128
BF16_SUBLANES = 16
MIB = 1 << 20
V7X_VMEM_BYTES = 64 * MIB
VMEM_LIMIT_CAP = V7X_VMEM_BYTES - 4 * MIB
INTERNAL_SCRATCH_BYTES = 6 * MIB
LOG2E = math.log2(math.e)
UNDERFLOW_BITS = 150.0
OUT_OF_KEYS_CARRY = 1e30
Q_ROWS = 64
LEAD_BACK = 192


def _tile(n, target, align):
    if n <= target:
        return n
    t = (target // align) * align
    while t >= align:
        if n % t == 0:
            return t
        t -= align
    raise ValueError(f"no tile for n={n} target={target} align={align}")


def _nbytes(shape, dtype):
    return math.prod(shape) * jnp.dtype(dtype).itemsize


def _params(semantics, pipelined_bytes, resident_bytes=0):
    need = 2 * pipelined_bytes + resident_bytes + INTERNAL_SCRATCH_BYTES
    return pltpu.CompilerParams(
        dimension_semantics=semantics,
        vmem_limit_bytes=int(min(VMEM_LIMIT_CAP, max(need, 16 * MIB))),
    )


def _rms_scale(x):
    return lax.rsqrt(jnp.mean(x * x, axis=-1, keepdims=True) + RMS_EPS)


def _rmsnorm_kernel(x_ref, g_ref, o_ref):
    x = x_ref[...]
    o_ref[...] = (x * _rms_scale(x) * g_ref[...]).astype(o_ref.dtype)


def _rmsnorm(x, g):
    m, d = x.shape
    tm = _tile(m, 512, 8)
    return pl.pallas_call(
        _rmsnorm_kernel,
        grid=(m // tm,),
        in_specs=[pl.BlockSpec((tm, d), lambda i: (i, 0)), pl.BlockSpec((1, d), lambda i: (0, 0))],
        out_specs=pl.BlockSpec((tm, d), lambda i: (i, 0)),
        out_shape=jax.ShapeDtypeStruct((m, d), BF16),
        compiler_params=_params(("parallel",), _nbytes((tm, d), F32) + _nbytes((tm, d), BF16),
                                _nbytes((tm, d), F32)),
        name="rmsnorm",
    )(x, g)


def _matmul_kernel(a_ref, w_ref, o_ref):
    o_ref[...] = jnp.dot(a_ref[...], w_ref[...], preferred_element_type=F32).astype(o_ref.dtype)


def _in_proj_rows(xn, w, col0, ncols):
    m, k = xn.shape
    tm = _tile(m, 1024, 8)
    tn = _tile(math.gcd(ncols, col0) if col0 else ncols, 1024, LANES)
    off = col0 // tn
    blocks = _nbytes((tm, k), BF16) + _nbytes((k, tn), BF16) + _nbytes((tm, tn), BF16)
    return pl.pallas_call(
        _matmul_kernel,
        grid=(m // tm, ncols // tn),
        in_specs=[pl.BlockSpec((tm, k), lambda i, j: (i, 0)),
                  pl.BlockSpec((k, tn), lambda i, j: (0, j + off))],
        out_specs=pl.BlockSpec((tm, tn), lambda i, j: (i, j)),
        out_shape=jax.ShapeDtypeStruct((m, ncols), BF16),
        compiler_params=_params(("parallel", "arbitrary"), blocks, _nbytes((tm, tn), F32)),
        name="in_proj_rows",
    )(xn, w)


def _matmul_heads_kernel(a_ref, w_ref, o_ref, *, heads_per_tile):
    res = jnp.dot(a_ref[...], w_ref[...], preferred_element_type=F32)
    for hh in range(heads_per_tile):
        o_ref[hh] = res[:, hh * HEAD_DIM:(hh + 1) * HEAD_DIM].astype(o_ref.dtype)


def _in_proj_heads(xn, w, col0, ncols, batch, seq):
    m, k = xn.shape
    tm = _tile(seq, 1024, 8)
    tn = _tile(math.gcd(ncols, col0), 1024, LANES)
    off = col0 // tn
    hpt = tn // HEAD_DIM
    spt = seq // tm
    blocks = _nbytes((tm, k), BF16) + _nbytes((k, tn), BF16) + _nbytes((tm, tn), BF16)
    return pl.pallas_call(
        functools.partial(_matmul_heads_kernel, heads_per_tile=hpt),
        grid=(m // tm, ncols // tn),
        in_specs=[pl.BlockSpec((tm, k), lambda i, j: (i, 0)),
                  pl.BlockSpec((k, tn), lambda i, j: (0, j + off))],
        out_specs=pl.BlockSpec((None, hpt, tm, HEAD_DIM), lambda i, j: (i // spt, j, i % spt, 0)),
        out_shape=jax.ShapeDtypeStruct((batch, ncols // HEAD_DIM, seq, HEAD_DIM), BF16),
        compiler_params=_params(("parallel", "arbitrary"), blocks, _nbytes((tm, tn), F32)),
        name="in_proj_heads",
    )(xn, w)


def _pool_kernel(a_ref, halo_ref, pw_ref, ps_ref, o_ref, ext_ref, *, seq, tm, group):
    t0 = lax.rem(pl.program_id(0) * tm, seq)
    a = a_ref[...].astype(F32)
    ext_ref[0:POOL_HALO, :] = jnp.where(t0 == 0, 0.0, halo_ref[...].astype(F32))
    ext_ref[POOL_HALO:, :] = a
    pos = t0 + lax.broadcasted_iota(jnp.int32, (tm, 1), 0)
    for g, w in enumerate(POOL_WINDOWS):
        cols = slice(g * group, (g + 1) * group)
        win = a[:, cols]
        for back in range(1, w):
            win = win + ext_ref[POOL_HALO - back:POOL_HALO - back + tm, cols]
        cnt = jnp.minimum(pos + 1, w).astype(F32)
        pooled = win / cnt - a[:, cols]
        y = jnp.dot(pooled.astype(BF16), pw_ref[g], preferred_element_type=F32)
        o_ref[:, cols] = (y * ps_ref[:, cols]).astype(o_ref.dtype)


def _pool_mixer(z_rows, pool_w, pool_scale, seq):
    m = z_rows.shape[0]
    n_groups, group, _ = pool_w.shape
    width = n_groups * group
    tm = _tile(seq, 512, POOL_HALO)
    hpt = tm // POOL_HALO
    blocks = (_nbytes((tm, width), BF16) * 2 + _nbytes((POOL_HALO, width), BF16)
              + _nbytes(pool_w.shape, BF16) + _nbytes((1, width), F32))
    return pl.pallas_call(
        functools.partial(_pool_kernel, seq=seq, tm=tm, group=group),
        grid=(m // tm,),
        in_specs=[pl.BlockSpec((tm, width), lambda i: (i, 0)),
                  pl.BlockSpec((POOL_HALO, width), lambda i: (jnp.maximum(i * hpt - 1, 0), 0)),
                  pl.BlockSpec(pool_w.shape, lambda i: (0, 0, 0)),
                  pl.BlockSpec((1, width), lambda i: (0, 0))],
        out_specs=pl.BlockSpec((tm, width), lambda i: (i, 0)),
        out_shape=jax.ShapeDtypeStruct((m, width), BF16),
        scratch_shapes=[pltpu.VMEM((tm + POOL_HALO, width), F32)],
        compiler_params=_params(("parallel",), blocks, 4 * _nbytes((tm + POOL_HALO, width), F32)),
        name="pool_mixer",
    )(z_rows, z_rows, pool_w, pool_scale)


def _gelu_tanh(x):
    return x * (0.5 * (1.0 + jnp.tanh(math.sqrt(2.0 / math.pi) * (x + 0.044715 * (x * x * x)))))


def _sgu_kernel(u_ref, v_ref, nw_ref, ws_ref, b_ref, o_ref, *, heads, chunks):
    v = _gelu_tanh(v_ref[...].astype(F32))
    vc = v - jnp.mean(v, axis=-1, keepdims=True)
    var = jnp.mean(vc * vc, axis=-1, keepdims=True)
    vn = (vc * lax.rsqrt(var + LN_EPS) * nw_ref[...]).astype(BF16)
    row = lax.broadcasted_iota(jnp.int32, (HEAD_DIM, HEAD_DIM), 0)
    col = lax.broadcasted_iota(jnp.int32, (HEAD_DIM, HEAD_DIM), 1)
    causal = row >= col
    for h in range(heads):
        cols = slice(h * HEAD_DIM, (h + 1) * HEAD_DIM)
        w = jnp.where(causal, ws_ref[h], 0.0).astype(BF16)
        bias = b_ref[:, h:h + 1]
        for c in range(chunks):
            rows = slice(c * HEAD_DIM, (c + 1) * HEAD_DIM)
            mixed = jnp.dot(w, vn[rows, cols], preferred_element_type=F32) + bias
            u = _gelu_tanh(u_ref[rows, cols].astype(F32))
            o_ref[rows, cols] = (u * mixed).astype(o_ref.dtype)


def _sgu_mixer(z_rows, col0, norm_w, w_s, b_t):
    m = z_rows.shape[0]
    heads = w_s.shape[0]
    width = heads * HEAD_DIM
    assert col0 % width == 0 and w_s.shape[1:] == (HEAD_DIM, HEAD_DIM)
    cb = col0 // width
    tm = _tile(m, 256, HEAD_DIM)
    blocks = (3 * _nbytes((tm, width), BF16) + _nbytes((1, width), F32) + _nbytes(w_s.shape, F32)
              + _nbytes((HEAD_DIM, LANES), F32))
    return pl.pallas_call(
        functools.partial(_sgu_kernel, heads=heads, chunks=tm // HEAD_DIM),
        grid=(m // tm,),
        in_specs=[pl.BlockSpec((tm, width), lambda i: (i, cb)),
                  pl.BlockSpec((tm, width), lambda i: (i, cb + 1)),
                  pl.BlockSpec((1, width), lambda i: (0, 0)),
                  pl.BlockSpec(w_s.shape, lambda i: (0, 0, 0)),
                  pl.BlockSpec(b_t.shape, lambda i: (0, 0))],
        out_specs=pl.BlockSpec((tm, width), lambda i: (i, 0)),
        out_shape=jax.ShapeDtypeStruct((m, width), BF16),
        compiler_params=_params(("parallel",), blocks, 4 * _nbytes((tm, width), F32)),
        name="sgu_mixer",
    )(z_rows, z_rows, norm_w, w_s, b_t)


def _neg_abs(x):
    bits = lax.bitcast_convert_type(x, jnp.uint32) | jnp.uint32(0x80000000)
    return lax.bitcast_convert_type(bits, F32)


def _bf16_split(x):
    bits = lax.bitcast_convert_type(x, jnp.uint32) & jnp.uint32(0xFFFF0000)
    hi = lax.bitcast_convert_type(bits, F32)
    return hi, x - hi


def _sb_streams(streams, cumsum_rhs, c2):
    scores = [lax.dot_general(q, kwin, (((1,), (1,)), ((), ())), preferred_element_type=F32)
              for q, kwin, _, _, _ in streams]
    log_betas, sums = [], []
    for s, (_, _, _, _, masks) in zip(scores, streams):
        z2 = s * c2
        sp = jnp.maximum(z2, 0.0) + jnp.log(1.0 + jnp.exp2(_neg_abs(z2))) * LOG2E
        log_betas.append(z2 - sp)
        lhs = []
        for j, mask in enumerate(masks):
            spj = sp[:, j * LANES:(j + 1) * LANES]
            if mask is not None:
                spj = jnp.where(mask, spj, 0.0)
            hi, lo = _bf16_split(spj)
            lhs.append(jnp.concatenate([hi.astype(BF16), lo.astype(BF16)], axis=1))
        lhs = lhs[0] if len(lhs) == 1 else jnp.concatenate(lhs, axis=0)
        sums.append(jnp.dot(lhs, cumsum_rhs, preferred_element_type=F32))
    out = []
    for (q, _, vwin, carry, masks), log_beta, both in zip(streams, log_betas, sums):
        r = q.shape[0]
        weights = [None] * len(masks)
        for j in reversed(range(len(masks))):
            inner = both[j * r:(j + 1) * r, :LANES]
            total = both[j * r:(j + 1) * r, LANES:]
            suffix = inner if carry is None else inner + carry
            pj = jnp.exp2(log_beta[:, j * LANES:(j + 1) * LANES] - suffix)
            if masks[j] is not None:
                pj = jnp.where(masks[j], pj, 0.0)
            weights[j] = pj.astype(BF16)
            carry = total if carry is None else carry + total
        p = weights[0] if len(weights) == 1 else jnp.concatenate(weights, axis=1)
        out.append((jnp.dot(p, vwin, preferred_element_type=F32), carry))
    return out


def _attn_kernel(q_ref, k_ref, v_ref, o_ref, *, heads, n_blocks, peel, group, c2):
    r = Q_ROWS
    krow = lax.broadcasted_iota(jnp.int32, (LANES, LANES), 0)
    kcol = lax.broadcasted_iota(jnp.int32, (LANES, LANES), 1)
    half = jnp.concatenate([(krow > kcol).astype(BF16), jnp.ones((LANES, LANES), BF16)], axis=1)
    cumsum_rhs = jnp.concatenate([half, half], axis=0)
    col = lax.broadcasted_iota(jnp.int32, (r, LANES), 1)
    ahead = col - lax.broadcasted_iota(jnp.int32, (r, LANES), 0)
    run = functools.partial(_sb_streams, cumsum_rhs=cumsum_rhs, c2=c2)

    def span(start, size):
        if isinstance(start, int):
            return slice(start, start + size)
        return pl.ds(pl.multiple_of(start, r), size)

    def run_blocks(blocks):
        def keys_back(b):
            return min(b * r, LEAD_BACK) if isinstance(b, int) else LEAD_BACK

        tasks = [(hh, b, keys_back(b)) for hh in range(heads) for b in blocks]
        streams = []
        for hh, b, back in tasks:
            n = -(-(back + r) // LANES)
            masks = [None if back - j * LANES >= LANES else ahead < back - j * LANES for j in range(n)]
            streams.append((q_ref[hh, span(b * r, r), :], k_ref[hh, span(b * r - back, n * LANES), :],
                            v_ref[hh, span(b * r - back, n * LANES), :], None, masks))
        done = run(streams)
        accs = tuple(acc for acc, _ in done)
        if any(not isinstance(b, int) or b * r - back > 0 for _, b, back in tasks):
            def left_edge(b, back, step):
                return b * r - back - step * LANES

            def cond(st):
                step, carries, _ = st
                least = None
                for (_, b, back), carry in zip(tasks, carries):
                    carry = carry + jnp.where(left_edge(b, back, step) <= 0, OUT_OF_KEYS_CARRY, 0.0)
                    least = carry if least is None else jnp.minimum(least, carry)
                return jnp.min(least) < UNDERFLOW_BITS

            def body(st):
                step, carries, accs = st
                more = []
                for (hh, b, back), carry in zip(tasks, carries):
                    edge = left_edge(b, back, step)
                    start = jnp.maximum(edge - LANES, 0)
                    carry = carry + jnp.where(edge <= 0, OUT_OF_KEYS_CARRY, 0.0)
                    more.append((q_ref[hh, span(b * r, r), :], k_ref[hh, span(start, LANES), :],
                                 v_ref[hh, span(start, LANES), :], carry, [col < edge - start]))
                parts = run(more)
                return (step + 1, tuple(carry for _, carry in parts),
                        tuple(acc + part for acc, (part, _) in zip(accs, parts)))

            _, _, accs = lax.while_loop(cond, body, (0, tuple(carry for _, carry in done), accs))
        for (hh, b, _), acc in zip(tasks, accs):
            o_ref[span(b * r, r), hh * HEAD_DIM:(hh + 1) * HEAD_DIM] = acc.astype(o_ref.dtype)

    for first in range(0, peel, group):
        run_blocks(list(range(first, min(first + group, peel))))

    def group_body(g, _):
        run_blocks([peel + g * group + t for t in range(group)])
        return 0

    if n_blocks > peel:
        lax.fori_loop(0, (n_blocks - peel) // group, group_body, 0)


def _sb_attention(z_heads, n_heads, batch, seq):
    heads = math.gcd(n_heads, 4)
    n_blocks = seq // Q_ROWS
    min_peel = min(-(-LEAD_BACK // Q_ROWS), n_blocks)
    group = min(4, max(n_blocks - min_peel, 1))
    peel = min_peel + (n_blocks - min_peel) % group
    hb = n_heads // heads
    blocks = 4 * _nbytes((heads, seq, HEAD_DIM), BF16)
    kern = functools.partial(_attn_kernel, heads=heads, n_blocks=n_blocks, peel=peel, group=group,
                             c2=HEAD_DIM ** -0.5 * LOG2E)
    return pl.pallas_call(
        kern,
        grid=(batch, hb),
        in_specs=[pl.BlockSpec((None, heads, seq, HEAD_DIM), lambda b, h: (b, h, 0, 0)),
                  pl.BlockSpec((None, heads, seq, HEAD_DIM), lambda b, h: (b, hb + h, 0, 0)),
                  pl.BlockSpec((None, heads, seq, HEAD_DIM), lambda b, h: (b, 2 * hb + h, 0, 0))],
        out_specs=pl.BlockSpec((None, seq, heads * HEAD_DIM), lambda b, h: (b, 0, h)),
        out_shape=jax.ShapeDtypeStruct((batch, seq, n_heads * HEAD_DIM), BF16),
        compiler_params=_params(("parallel", "parallel"), blocks, 8 * MIB),
        name="sb_attention",
    )(z_heads, z_heads, z_heads)


def _out_proj_kernel(yp_ref, ys_ref, yb_ref, w_ref, h_ref, g_ref, ho_ref, hg_ref, rs_ref,
                     ssq_ref, *, pw, sw, nj, d):
    j = pl.program_id(1)

    @pl.when(j == 0)
    def _():
        ssq_ref[...] = jnp.zeros_like(ssq_ref)

    acc = jnp.dot(yp_ref[...], w_ref[0:pw, :], preferred_element_type=F32)
    acc += jnp.dot(ys_ref[...], w_ref[pw:pw + sw, :], preferred_element_type=F32)
    acc += jnp.dot(yb_ref[...], w_ref[pw + sw:, :], preferred_element_type=F32)
    hn = h_ref[...] + acc
    ho_ref[...] = hn
    hg_ref[...] = (hn * g_ref[...]).astype(hg_ref.dtype)
    ssq_ref[...] += jnp.sum(hn * hn, axis=-1, keepdims=True)

    @pl.when(j == nj - 1)
    def _():
        rs_ref[...] = lax.rsqrt(ssq_ref[...] * (1.0 / d) + RMS_EPS)


def _out_proj(h, y_pool, y_sgu, y_sb, w_out, g_next):
    m, d = h.shape
    pw, sw, bw = y_pool.shape[1], y_sgu.shape[1], y_sb.shape[1]
    k = pw + sw + bw
    tm = _tile(m, 1024, 8)
    tn = _tile(d, 512, LANES)
    nj = d // tn
    blocks = (_nbytes((tm, k), BF16) + _nbytes((k, tn), BF16) + 2 * _nbytes((tm, tn), F32)
              + _nbytes((8, tn), F32) + _nbytes((tm, tn), BF16) + _nbytes((tm, LANES), F32))
    return pl.pallas_call(
        functools.partial(_out_proj_kernel, pw=pw, sw=sw, nj=nj, d=d),
        grid=(m // tm, nj),
        in_specs=[pl.BlockSpec((tm, pw), lambda i, j: (i, 0)),
                  pl.BlockSpec((tm, sw), lambda i, j: (i, 0)),
                  pl.BlockSpec((tm, bw), lambda i, j: (i, 0)),
                  pl.BlockSpec((k, tn), lambda i, j: (0, j)),
                  pl.BlockSpec((tm, tn), lambda i, j: (i, j)),
                  pl.BlockSpec((1, tn), lambda i, j: (0, j))],
        out_specs=[pl.BlockSpec((tm, tn), lambda i, j: (i, j)),
                   pl.BlockSpec((tm, tn), lambda i, j: (i, j)),
                   pl.BlockSpec((tm, 1), lambda i, j: (i, 0))],
        out_shape=[jax.ShapeDtypeStruct((m, d), F32), jax.ShapeDtypeStruct((m, d), BF16),
                   jax.ShapeDtypeStruct((m, 1), F32)],
        scratch_shapes=[pltpu.VMEM((tm, 1), F32)],
        compiler_params=_params(("parallel", "arbitrary"), blocks,
                                _nbytes((tm, LANES), F32) + 3 * _nbytes((tm, tn), F32)),
        name="out_proj",
    )(y_pool, y_sgu, y_sb, w_out, h, g_next)


class _CastJobs:
    def __init__(self, jobs, grid):
        self.jobs = jobs
        self.plans = [self.plan(w.shape[1:], grid) for w, _ in jobs]
        assert all(p is not None for p in self.plans)

    @staticmethod
    def plan(shape, grid):
        rows, cols = shape
        for rows_on_first in (True, False):
            gr, gc = grid if rows_on_first else grid[::-1]
            if rows % gr == 0 and cols % gc == 0:
                block = (rows // gr, cols // gc)
                if block[0] % BF16_SUBLANES == 0 and block[1] % LANES == 0:
                    return block, rows_on_first
        return None

    def __len__(self):
        return len(self.jobs)

    def arrays(self):
        return [w for w, _ in self.jobs]

    def in_specs(self):
        return [pl.BlockSpec((None,) + block,
                             (lambda i, j, layer=layer: (layer, i, j)) if first else
                             (lambda i, j, layer=layer: (layer, j, i)))
                for (_, layer), (block, first) in zip(self.jobs, self.plans)]

    def out_specs(self):
        return [pl.BlockSpec(block, (lambda i, j: (i, j)) if first else (lambda i, j: (j, i)))
                for block, first in self.plans]

    def out_shapes(self):
        return [jax.ShapeDtypeStruct(w.shape[1:], BF16) for w, _ in self.jobs]

    def block_bytes(self):
        return sum(_nbytes(block, F32) + _nbytes(block, BF16) for block, _ in self.plans)


def _cast_kernel(w_ref, o_ref):
    o_ref[...] = w_ref[...].astype(o_ref.dtype)


def _cast_layer(w, layer):
    _, rows, cols = w.shape
    tr = _tile(rows, 128, BF16_SUBLANES)
    return pl.pallas_call(
        _cast_kernel,
        grid=(rows // tr,),
        in_specs=[pl.BlockSpec((None, tr, cols), lambda i: (layer, i, 0))],
        out_specs=pl.BlockSpec((tr, cols), lambda i: (i, 0)),
        out_shape=jax.ShapeDtypeStruct((rows, cols), BF16),
        compiler_params=_params(("parallel",), _nbytes((tr, cols), F32) + _nbytes((tr, cols), BF16)),
        name="cast_layer",
    )(w)


def _run_casts(refs, n):
    for src, dst in zip(refs[:n], refs[len(refs) - n:]):
        dst[...] = src[...].astype(dst.dtype)


def _gate_up_kernel(x_ref, rs_ref, wg_ref, wu_ref, *refs, chunks, n_casts):
    o_ref = refs[n_casts]
    rows_per_chunk = x_ref.shape[0] // chunks
    for c in range(chunks):
        rows = slice(c * rows_per_chunk, (c + 1) * rows_per_chunk)
        x = x_ref[rows, :]
        rs = rs_ref[rows, :]
        g = jnp.dot(x, wg_ref[...], preferred_element_type=F32) * rs
        u = jnp.dot(x, wu_ref[...], preferred_element_type=F32) * rs
        o_ref[rows, :] = (g * jax.nn.sigmoid(g) * u).astype(o_ref.dtype)
    _run_casts(refs, n_casts)


def _gate_up_grid(m, f):
    tm = _tile(m, 2048, 8)
    tf = _tile(f, 512, LANES)
    return tm, tf, (m // tm, f // tf)


def _gate_up(hg, rs, w_gate, w_up, cast_jobs):
    m, k = hg.shape
    f = w_gate.shape[1]
    tm, tf, grid = _gate_up_grid(m, f)
    casts = _CastJobs(cast_jobs, grid)
    blocks = (_nbytes((tm, k), BF16) + _nbytes((tm, LANES), F32) + 2 * _nbytes((k, tf), BF16)
              + _nbytes((tm, tf), BF16) + casts.block_bytes())
    return pl.pallas_call(
        functools.partial(_gate_up_kernel, chunks=2 if tm % 1024 == 0 else 1, n_casts=len(casts)),
        grid=grid,
        in_specs=[pl.BlockSpec((tm, k), lambda i, j: (i, 0)),
                  pl.BlockSpec((tm, 1), lambda i, j: (i, 0)),
                  pl.BlockSpec((k, tf), lambda i, j: (0, j)),
                  pl.BlockSpec((k, tf), lambda i, j: (0, j))] + casts.in_specs(),
        out_specs=[pl.BlockSpec((tm, tf), lambda i, j: (i, j))] + casts.out_specs(),
        out_shape=[jax.ShapeDtypeStruct((m, f), BF16)] + casts.out_shapes(),
        compiler_params=_params(("parallel", "arbitrary"), blocks, 3 * _nbytes((tm, tf), F32)),
        name="ffn_gate_up",
    )(hg, rs, w_gate, w_up, *casts.arrays())


def _down_kernel(a_ref, w_ref, h_ref, *refs, n_casts):
    o_ref = refs[n_casts]
    o_ref[...] = h_ref[...] + jnp.dot(a_ref[...], w_ref[...], preferred_element_type=F32)
    _run_casts(refs, n_casts)


def _down_grid(m, d):
    tm = _tile(m, 512, 8)
    tn = _tile(d, 512, LANES)
    return tm, tn, (m // tm, d // tn)


def _down_proj(h, act, w_down, cast_jobs):
    m, d = h.shape
    f = act.shape[1]
    tm, tn, grid = _down_grid(m, d)
    casts = _CastJobs(cast_jobs, grid)
    blocks = (_nbytes((tm, f), BF16) + _nbytes((f, tn), BF16) + 2 * _nbytes((tm, tn), F32)
              + casts.block_bytes())
    return pl.pallas_call(
        functools.partial(_down_kernel, n_casts=len(casts)),
        grid=grid,
        in_specs=[pl.BlockSpec((tm, f), lambda i, j: (i, 0)),
                  pl.BlockSpec((f, tn), lambda i, j: (0, j)),
                  pl.BlockSpec((tm, tn), lambda i, j: (i, j))] + casts.in_specs(),
        out_specs=[pl.BlockSpec((tm, tn), lambda i, j: (i, j))] + casts.out_specs(),
        out_shape=[jax.ShapeDtypeStruct((m, d), F32)] + casts.out_shapes(),
        compiler_params=_params(("parallel", "arbitrary"), blocks, _nbytes((tm, tn), F32)),
        name="ffn_down",
    )(act, w_down, h, *casts.arrays())


def _ple_kernel(h_ref, p_ref, gn_ref, gd_ref, gu_ref, pp_ref, gnext_ref, *out_refs):
    h = h_ref[...]
    gn = (h * _rms_scale(h) * gn_ref[...]).astype(BF16)
    low = jnp.dot(gn, gd_ref[...], preferred_element_type=F32).astype(BF16)
    gate = jax.nn.sigmoid(jnp.dot(low, gu_ref[...], preferred_element_type=F32))
    emb = jnp.dot(p_ref[...].astype(BF16), pp_ref[...], preferred_element_type=F32)
    hn = h + gate * emb
    xn_ref = out_refs[-1]
    if len(out_refs) == 2:
        out_refs[0][...] = hn
    xn_ref[...] = (hn * _rms_scale(hn) * gnext_ref[...]).astype(xn_ref.dtype)


def _ple(h, p, g_ple, gate_down, gate_up, proj, g_next, last):
    m, d = h.shape
    e = p.shape[1]
    tm = _tile(m, 256, 8)
    row = pl.BlockSpec((tm, d), lambda i: (i, 0))
    vec = pl.BlockSpec((1, d), lambda i: (0, 0))
    if last:
        out_specs = [row]
        out_shape = [jax.ShapeDtypeStruct((m, d), F32)]
        out_bytes = _nbytes((tm, d), F32)
    else:
        out_specs = [row, row]
        out_shape = [jax.ShapeDtypeStruct((m, d), F32), jax.ShapeDtypeStruct((m, d), BF16)]
        out_bytes = _nbytes((tm, d), F32) + _nbytes((tm, d), BF16)
    blocks = (_nbytes((tm, d), F32) + _nbytes((tm, e), F32) + 2 * _nbytes((1, d), F32)
              + 3 * _nbytes((d, e), BF16) + out_bytes)
    return pl.pallas_call(
        _ple_kernel,
        grid=(m // tm,),
        in_specs=[row,
                  pl.BlockSpec((tm, e), lambda i: (i, 0)),
                  vec,
                  pl.BlockSpec((d, e), lambda i: (0, 0)),
                  pl.BlockSpec((e, d), lambda i: (0, 0)),
                  pl.BlockSpec((e, d), lambda i: (0, 0)),
                  vec],
        out_specs=out_specs,
        out_shape=out_shape,
        compiler_params=_params(("parallel",), blocks, 4 * _nbytes((tm, d), F32)),
        name="ple_gate",
    )(h, p, g_ple, gate_down, gate_up, proj, g_next)


def kernel(x, p, norm_mix_w, w_in, pool_w, pool_scale, sgu_norm_w, sgu_w, sgu_b, w_out, norm_ffn_w,
           w_gate, w_up, w_down, norm_ple_w, ple_gate_down, ple_gate_up, ple_proj, final_norm_w):
    batch, seq, d = x.shape
    depth = w_in.shape[0]
    m = batch * seq
    pool_width = pool_scale.shape[-1]
    sgu_width = sgu_norm_w.shape[-1]
    sb_width = w_out.shape[1] - pool_width - sgu_width
    n_heads = sb_width // HEAD_DIM
    rows_width = pool_width + 2 * sgu_width
    assert w_in.shape[2] == rows_width + 3 * sb_width and sb_width % HEAD_DIM == 0
    assert seq % LANES == 0 and sgu_w.shape[1] == sgu_width // HEAD_DIM
    assert pool_w.shape[1] == len(POOL_WINDOWS) and pool_w.shape[1] * pool_w.shape[2] == pool_width

    f = w_gate.shape[2]
    gate_up_grid, down_grid = _gate_up_grid(m, f)[2], _down_grid(m, d)[2]
    ride_gate_up = all(_CastJobs.plan(w.shape[1:], gate_up_grid) for w in (w_down, w_gate, w_up))
    ride_down = all(_CastJobs.plan(w.shape[1:], down_grid) for w in (w_in, w_out))

    w_in_b, w_out_b, w_gate_b, w_up_b = (_cast_layer(w, 0) for w in (w_in, w_out, w_gate, w_up))
    h = x.reshape(m, d)
    xn = _rmsnorm(h, norm_mix_w[0].reshape(1, d))
    out = None
    for i in range(depth):
        more = i + 1 < depth
        z_rows = _in_proj_rows(xn, w_in_b, 0, rows_width)
        z_heads = _in_proj_heads(xn, w_in_b, rows_width, 3 * sb_width, batch, seq)
        y_pool = _pool_mixer(z_rows, pool_w[i].astype(BF16), pool_scale[i].reshape(1, pool_width), seq)
        y_sgu = _sgu_mixer(z_rows, pool_width, sgu_norm_w[i].reshape(1, sgu_width), sgu_w[i],
                           sgu_b[i].T)
        y_sb = _sb_attention(z_heads, n_heads, batch, seq).reshape(m, sb_width)
        h, hg, rs = _out_proj(h, y_pool, y_sgu, y_sb, w_out_b, norm_ffn_w[i].reshape(1, d))

        jobs = [(w_down, i)] + ([(w_gate, i + 1), (w_up, i + 1)] if more else [])
        act, *cast = _gate_up(hg, rs, w_gate_b, w_up_b, jobs if ride_gate_up else [])
        if ride_gate_up:
            w_down_b = cast[0]
            if more:
                w_gate_b, w_up_b = cast[1], cast[2]
        else:
            w_down_b = _cast_layer(w_down, i)
            if more:
                w_gate_b, w_up_b = _cast_layer(w_gate, i + 1), _cast_layer(w_up, i + 1)

        jobs = [(w_in, i + 1), (w_out, i + 1)] if more and ride_down else []
        h, *cast = _down_proj(h, act, w_down_b, jobs)
        if jobs:
            w_in_b, w_out_b = cast
        elif more:
            w_in_b, w_out_b = _cast_layer(w_in, i + 1), _cast_layer(w_out, i + 1)
        last = i == depth - 1
        g_next = final_norm_w if last else norm_mix_w[i + 1]
        res = _ple(h, p[i].reshape(m, -1), norm_ple_w[i].reshape(1, d), ple_gate_down[i].astype(BF16),
                   ple_gate_up[i].astype(BF16), ple_proj[i].astype(BF16), g_next.reshape(1, d), last)
        if last:
            out = res[0]
        else:
            h, xn = res
    return out.reshape(batch, seq, d)
```

```python
import functools
import math

import jax
import jax.numpy as jnp
from jax import lax
from jax.experimental import pallas as pl
from jax.experimental.pallas import tpu as pltpu

F32 = jnp.float32
BF16 = jnp.bfloat16

RMS_EPS = 1e-6
LN_EPS = 1e-5
POOL_WINDOWS = (2, 4, 8, 16)
POOL_HALO = 16
HEAD_DIM = 128
LANES = 128
BF16_SUBLANES = 16
MIB = 1 << 20
V7X_VMEM_BYTES = 64 * MIB
VMEM_LIMIT_CAP = V7X_VMEM_BYTES - 4 * MIB
INTERNAL_SCRATCH_BYTES = 6 * MIB
LOG2E = math.log2(math.e)
UNDERFLOW_BITS = 150.0
OUT_OF_KEYS_CARRY = 1e30
Q_ROWS = 64
LEAD_BACK = 192


def _tile(n, target, align):
    if n <= target:
        return n
    t = (target // align) * align
    while t >= align:
        if n % t == 0:
            return t
        t -= align
    raise ValueError(f"no tile for n={n} target={target} align={align}")


def _nbytes(shape, dtype):
    return math.prod(shape) * jnp.dtype(dtype).itemsize


def _params(semantics, pipelined_bytes, resident_bytes=0):
    need = 2 * pipelined_bytes + resident_bytes + INTERNAL_SCRATCH_BYTES
    return pltpu.CompilerParams(
        dimension_semantics=semantics,
        vmem_limit_bytes=int(min(VMEM_LIMIT_CAP, max(need, 16 * MIB))),
    )


def _rms_scale(x):
    return lax.rsqrt(jnp.mean(x * x, axis=-1, keepdims=True) + RMS_EPS)


def _rmsnorm_kernel(x_ref, g_ref, o_ref):
    x = x_ref[...]
    o_ref[...] = (x * _rms_scale(x) * g_ref[...]).astype(o_ref.dtype)


def _rmsnorm(x, g):
    m, d = x.shape
    tm = _tile(m, 512, 8)
    return pl.pallas_call(
        _rmsnorm_kernel,
        grid=(m // tm,),
        in_specs=[pl.BlockSpec((tm, d), lambda i: (i, 0)), pl.BlockSpec((1, d), lambda i: (0, 0))],
        out_specs=pl.BlockSpec((tm, d), lambda i: (i, 0)),
        out_shape=jax.ShapeDtypeStruct((m, d), BF16),
        compiler_params=_params(("parallel",), _nbytes((tm, d), F32) + _nbytes((tm, d), BF16),
                                _nbytes((tm, d), F32)),
        name="rmsnorm",
    )(x, g)


def _matmul_kernel(a_ref, w_ref, o_ref):
    o_ref[...] = jnp.dot(a_ref[...], w_ref[...], preferred_element_type=F32).astype(o_ref.dtype)


def _in_proj_rows(xn, w, col0, ncols):
    m, k = xn.shape
    tm = _tile(m, 1024, 8)
    tn = _tile(math.gcd(ncols, col0) if col0 else ncols, 1024, LANES)
    off = col0 // tn
    blocks = _nbytes((tm, k), BF16) + _nbytes((k, tn), BF16) + _nbytes((tm, tn), BF16)
    return pl.pallas_call(
        _matmul_kernel,
        grid=(m // tm, ncols // tn),
        in_specs=[pl.BlockSpec((tm, k), lambda i, j: (i, 0)),
                  pl.BlockSpec((k, tn), lambda i, j: (0, j + off))],
        out_specs=pl.BlockSpec((tm, tn), lambda i, j: (i, j)),
        out_shape=jax.ShapeDtypeStruct((m, ncols), BF16),
        compiler_params=_params(("parallel", "arbitrary"), blocks, _nbytes((tm, tn), F32)),
        name="in_proj_rows",
    )(xn, w)


def _matmul_heads_kernel(a_ref, w_ref, o_ref, *, heads_per_tile):
    res = jnp.dot(a_ref[...], w_ref[...], preferred_element_type=F32)
    for hh in range(heads_per_tile):
        o_ref[hh] = res[:, hh * HEAD_DIM:(hh + 1) * HEAD_DIM].astype(o_ref.dtype)


def _in_proj_heads(xn, w, col0, ncols, batch, seq):
    m, k = xn.shape
    tm = _tile(seq, 1024, 8)
    tn = _tile(math.gcd(ncols, col0), 1024, LANES)
    off = col0 // tn
    hpt = tn // HEAD_DIM
    spt = seq // tm
    blocks = _nbytes((tm, k), BF16) + _nbytes((k, tn), BF16) + _nbytes((tm, tn), BF16)
    return pl.pallas_call(
        functools.partial(_matmul_heads_kernel, heads_per_tile=hpt),
        grid=(m // tm, ncols // tn),
        in_specs=[pl.BlockSpec((tm, k), lambda i, j: (i, 0)),
                  pl.BlockSpec((k, tn), lambda i, j: (0, j + off))],
        out_specs=pl.BlockSpec((None, hpt, tm, HEAD_DIM), lambda i, j: (i // spt, j, i % spt, 0)),
        out_shape=jax.ShapeDtypeStruct((batch, ncols // HEAD_DIM, seq, HEAD_DIM), BF16),
        compiler_params=_params(("parallel", "arbitrary"), blocks, _nbytes((tm, tn), F32)),
        name="in_proj_heads",
    )(xn, w)


def _pool_kernel(a_ref, halo_ref, pw_ref, ps_ref, o_ref, ext_ref, *, seq, tm, group):
    t0 = lax.rem(pl.program_id(0) * tm, seq)
    a = a_ref[...].astype(F32)
    ext_ref[0:POOL_HALO, :] = jnp.where(t0 == 0, 0.0, halo_ref[...].astype(F32))
    ext_ref[POOL_HALO:, :] = a
    pos = t0 + lax.broadcasted_iota(jnp.int32, (tm, 1), 0)
    for g, w in enumerate(POOL_WINDOWS):
        cols = slice(g * group, (g + 1) * group)
        win = a[:, cols]
        for back in range(1, w):
            win = win + ext_ref[POOL_HALO - back:POOL_HALO - back + tm, cols]
        cnt = jnp.minimum(pos + 1, w).astype(F32)
        pooled = win / cnt - a[:, cols]
        y = jnp.dot(pooled.astype(BF16), pw_ref[g], preferred_element_type=F32)
        o_ref[:, cols] = (y * ps_ref[:, cols]).astype(o_ref.dtype)


def _pool_mixer(z_rows, pool_w, pool_scale, seq):
    m = z_rows.shape[0]
    n_groups, group, _ = pool_w.shape
    width = n_groups * group
    tm = _tile(seq, 512, POOL_HALO)
    hpt = tm // POOL_HALO
    blocks = (_nbytes((tm, width), BF16) * 2 + _nbytes((POOL_HALO, width), BF16)
              + _nbytes(pool_w.shape, BF16) + _nbytes((1, width), F32))
    return pl.pallas_call(
        functools.partial(_pool_kernel, seq=seq, tm=tm, group=group),
        grid=(m // tm,),
        in_specs=[pl.BlockSpec((tm, width), lambda i: (i, 0)),
                  pl.BlockSpec((POOL_HALO, width), lambda i: (jnp.maximum(i * hpt - 1, 0), 0)),
                  pl.BlockSpec(pool_w.shape, lambda i: (0, 0, 0)),
                  pl.BlockSpec((1, width), lambda i: (0, 0))],
        out_specs=pl.BlockSpec((tm, width), lambda i: (i, 0)),
        out_shape=jax.ShapeDtypeStruct((m, width), BF16),
        scratch_shapes=[pltpu.VMEM((tm + POOL_HALO, width), F32)],
        compiler_params=_params(("parallel",), blocks, 4 * _nbytes((tm + POOL_HALO, width), F32)),
        name="pool_mixer",
    )(z_rows, z_rows, pool_w, pool_scale)


def _gelu_tanh(x):
    return x * (0.5 * (1.0 + jnp.tanh(math.sqrt(2.0 / math.pi) * (x + 0.044715 * (x * x * x)))))


def _sgu_kernel(u_ref, v_ref, nw_ref, ws_ref, b_ref, o_ref, *, heads, chunks):
    v = _gelu_tanh(v_ref[...].astype(F32))
    vc = v - jnp.mean(v, axis=-1, keepdims=True)
    var = jnp.mean(vc * vc, axis=-1, keepdims=True)
    vn = (vc * lax.rsqrt(var + LN_EPS) * nw_ref[...]).astype(BF16)
    row = lax.broadcasted_iota(jnp.int32, (HEAD_DIM, HEAD_DIM), 0)
    col = lax.broadcasted_iota(jnp.int32, (HEAD_DIM, HEAD_DIM), 1)
    causal = row >= col
    for h in range(heads):
        cols = slice(h * HEAD_DIM, (h + 1) * HEAD_DIM)
        w = jnp.where(causal, ws_ref[h], 0.0).astype(BF16)
        bias = b_ref[:, h:h + 1]
        for c in range(chunks):
            rows = slice(c * HEAD_DIM, (c + 1) * HEAD_DIM)
            mixed = jnp.dot(w, vn[rows, cols], preferred_element_type=F32) + bias
            u = _gelu_tanh(u_ref[rows, cols].astype(F32))
            o_ref[rows, cols] = (u * mixed).astype(o_ref.dtype)


def _sgu_mixer(z_rows, col0, norm_w, w_s, b_t):
    m = z_rows.shape[0]
    heads = w_s.shape[0]
    width = heads * HEAD_DIM
    assert col0 % width == 0 and w_s.shape[1:] == (HEAD_DIM, HEAD_DIM)
    cb = col0 // width
    tm = _tile(m, 256, HEAD_DIM)
    blocks = (3 * _nbytes((tm, width), BF16) + _nbytes((1, width), F32) + _nbytes(w_s.shape, F32)
              + _nbytes((HEAD_DIM, LANES), F32))
    return pl.pallas_call(
        functools.partial(_sgu_kernel, heads=heads, chunks=tm // HEAD_DIM),
        grid=(m // tm,),
        in_specs=[pl.BlockSpec((tm, width), lambda i: (i, cb)),
                  pl.BlockSpec((tm, width), lambda i: (i, cb + 1)),
                  pl.BlockSpec((1, width), lambda i: (0, 0)),
                  pl.BlockSpec(w_s.shape, lambda i: (0, 0, 0)),
                  pl.BlockSpec(b_t.shape, lambda i: (0, 0))],
        out_specs=pl.BlockSpec((tm, width), lambda i: (i, 0)),
        out_shape=jax.ShapeDtypeStruct((m, width), BF16),
        compiler_params=_params(("parallel",), blocks, 4 * _nbytes((tm, width), F32)),
        name="sgu_mixer",
    )(z_rows, z_rows, norm_w, w_s, b_t)


def _neg_abs(x):
    bits = lax.bitcast_convert_type(x, jnp.uint32) | jnp.uint32(0x80000000)
    return lax.bitcast_convert_type(bits, F32)


def _bf16_split(x):
    bits = lax.bitcast_convert_type(x, jnp.uint32) & jnp.uint32(0xFFFF0000)
    hi = lax.bitcast_convert_type(bits, F32)
    return hi, x - hi


def _sb_streams(streams, cumsum_rhs, c2):
    scores = [lax.dot_general(q, kwin, (((1,), (1,)), ((), ())), preferred_element_type=F32)
              for q, kwin, _, _, _ in streams]
    log_betas, sums = [], []
    for s, (_, _, _, _, masks) in zip(scores, streams):
        z2 = s * c2
        sp = jnp.maximum(z2, 0.0) + jnp.log(1.0 + jnp.exp2(_neg_abs(z2))) * LOG2E
        log_betas.append(z2 - sp)
        lhs = []
        for j, mask in enumerate(masks):
            spj = sp[:, j * LANES:(j + 1) * LANES]
            if mask is not None:
                spj = jnp.where(mask, spj, 0.0)
            hi, lo = _bf16_split(spj)
            lhs.append(jnp.concatenate([hi.astype(BF16), lo.astype(BF16)], axis=1))
        lhs = lhs[0] if len(lhs) == 1 else jnp.concatenate(lhs, axis=0)
        sums.append(jnp.dot(lhs, cumsum_rhs, preferred_element_type=F32))
    out = []
    for (q, _, vwin, carry, masks), log_beta, both in zip(streams, log_betas, sums):
        r = q.shape[0]
        weights = [None] * len(masks)
        for j in reversed(range(len(masks))):
            inner = both[j * r:(j + 1) * r, :LANES]
            total = both[j * r:(j + 1) * r, LANES:]
            suffix = inner if carry is None else inner + carry
            pj = jnp.exp2(log_beta[:, j * LANES:(j + 1) * LANES] - suffix)
            if masks[j] is not None:
                pj = jnp.where(masks[j], pj, 0.0)
            weights[j] = pj.astype(BF16)
            carry = total if carry is None else carry + total
        p = weights[0] if len(weights) == 1 else jnp.concatenate(weights, axis=1)
        out.append((jnp.dot(p, vwin, preferred_element_type=F32), carry))
    return out


def _attn_kernel(q_ref, k_ref, v_ref, o_ref, *, heads, n_blocks, peel, group, c2):
    r = Q_ROWS
    krow = lax.broadcasted_iota(jnp.int32, (LANES, LANES), 0)
    kcol = lax.broadcasted_iota(jnp.int32, (LANES, LANES), 1)
    half = jnp.concatenate([(krow > kcol).astype(BF16), jnp.ones((LANES, LANES), BF16)], axis=1)
    cumsum_rhs = jnp.concatenate([half, half], axis=0)
    col = lax.broadcasted_iota(jnp.int32, (r, LANES), 1)
    ahead = col - lax.broadcasted_iota(jnp.int32, (r, LANES), 0)
    run = functools.partial(_sb_streams, cumsum_rhs=cumsum_rhs, c2=c2)

    def span(start, size):
        if isinstance(start, int):
            return slice(start, start + size)
        return pl.ds(pl.multiple_of(start, r), size)

    def run_blocks(blocks):
        def keys_back(b):
            return min(b * r, LEAD_BACK) if isinstance(b, int) else LEAD_BACK

        tasks = [(hh, b, keys_back(b)) for hh in range(heads) for b in blocks]
        streams = []
        for hh, b, back in tasks:
            n = -(-(back + r) // LANES)
            masks = [None if back - j * LANES >= LANES else ahead < back - j * LANES for j in range(n)]
            streams.append((q_ref[hh, span(b * r, r), :], k_ref[hh, span(b * r - back, n * LANES), :],
                            v_ref[hh, span(b * r - back, n * LANES), :], None, masks))
        done = run(streams)
        accs = tuple(acc for acc, _ in done)
        if any(not isinstance(b, int) or b * r - back > 0 for _, b, back in tasks):
            def left_edge(b, back, step):
                return b * r - back - step * LANES

            def cond(st):
                step, carries, _ = st
                least = None
                for (_, b, back), carry in zip(tasks, carries):
                    carry = carry + jnp.where(left_edge(b, back, step) <= 0, OUT_OF_KEYS_CARRY, 0.0)
                    least = carry if least is None else jnp.minimum(least, carry)
                return jnp.min(least) < UNDERFLOW_BITS

            def body(st):
                step, carries, accs = st
                more = []
                for (hh, b, back), carry in zip(tasks, carries):
                    edge = left_edge(b, back, step)
                    start = jnp.maximum(edge - LANES, 0)
                    carry = carry + jnp.where(edge <= 0, OUT_OF_KEYS_CARRY, 0.0)
                    more.append((q_ref[hh, span(b * r, r), :], k_ref[hh, span(start, LANES), :],
                                 v_ref[hh, span(start, LANES), :], carry, [col < edge - start]))
                parts = run(more)
                return (step + 1, tuple(carry for _, carry in parts),
                        tuple(acc + part for acc, (part, _) in zip(accs, parts)))

            _, _, accs = lax.while_loop(cond, body, (0, tuple(carry for _, carry in done), accs))
        for (hh, b, _), acc in zip(tasks, accs):
            o_ref[span(b * r, r), hh * HEAD_DIM:(hh + 1) * HEAD_DIM] = acc.astype(o_ref.dtype)

    for first in range(0, peel, group):
        run_blocks(list(range(first, min(first + group, peel))))

    def group_body(g, _):
        run_blocks([peel + g * group + t for t in range(group)])
        return 0

    if n_blocks > peel:
        lax.fori_loop(0, (n_blocks - peel) // group, group_body, 0)


def _sb_attention(z_heads, n_heads, batch, seq):
    heads = math.gcd(n_heads, 4)
    n_blocks = seq // Q_ROWS
    min_peel = min(-(-LEAD_BACK // Q_ROWS), n_blocks)
    group = min(4, max(n_blocks - min_peel, 1))
    peel = min_peel + (n_blocks - min_peel) % group
    hb = n_heads // heads
    blocks = 4 * _nbytes((heads, seq, HEAD_DIM), BF16)
    kern = functools.partial(_attn_kernel, heads=heads, n_blocks=n_blocks, peel=peel, group=group,
                             c2=HEAD_DIM ** -0.5 * LOG2E)
    return pl.pallas_call(
        kern,
        grid=(batch, hb),
        in_specs=[pl.BlockSpec((None, heads, seq, HEAD_DIM), lambda b, h: (b, h, 0, 0)),
                  pl.BlockSpec((None, heads, seq, HEAD_DIM), lambda b, h: (b, hb + h, 0, 0)),
                  pl.BlockSpec((None, heads, seq, HEAD_DIM), lambda b, h: (b, 2 * hb + h, 0, 0))],
        out_specs=pl.BlockSpec((None, seq, heads * HEAD_DIM), lambda b, h: (b, 0, h)),
        out_shape=jax.ShapeDtypeStruct((batch, seq, n_heads * HEAD_DIM), BF16),
        compiler_params=_params(("parallel", "parallel"), blocks, 8 * MIB),
        name="sb_attention",
    )(z_heads, z_heads, z_heads)


def _out_proj_kernel(yp_ref, ys_ref, yb_ref, w_ref, h_ref, g_ref, ho_ref, hg_ref, rs_ref,
                     ssq_ref, *, pw, sw, nj, d):
    j = pl.program_id(1)

    @pl.when(j == 0)
    def _():
        ssq_ref[...] = jnp.zeros_like(ssq_ref)

    acc = jnp.dot(yp_ref[...], w_ref[0:pw, :], preferred_element_type=F32)
    acc += jnp.dot(ys_ref[...], w_ref[pw:pw + sw, :], preferred_element_type=F32)
    acc += jnp.dot(yb_ref[...], w_ref[pw + sw:, :], preferred_element_type=F32)
    hn = h_ref[...] + acc
    ho_ref[...] = hn
    hg_ref[...] = (hn * g_ref[...]).astype(hg_ref.dtype)
    ssq_ref[...] += jnp.sum(hn * hn, axis=-1, keepdims=True)

    @pl.when(j == nj - 1)
    def _():
        rs_ref[...] = lax.rsqrt(ssq_ref[...] * (1.0 / d) + RMS_EPS)


def _out_proj(h, y_pool, y_sgu, y_sb, w_out, g_next):
    m, d = h.shape
    pw, sw, bw = y_pool.shape[1], y_sgu.shape[1], y_sb.shape[1]
    k = pw + sw + bw
    tm = _tile(m, 1024, 8)
    tn = _tile(d, 512, LANES)
    nj = d // tn
    blocks = (_nbytes((tm, k), BF16) + _nbytes((k, tn), BF16) + 2 * _nbytes((tm, tn), F32)
              + _nbytes((8, tn), F32) + _nbytes((tm, tn), BF16) + _nbytes((tm, LANES), F32))
    return pl.pallas_call(
        functools.partial(_out_proj_kernel, pw=pw, sw=sw, nj=nj, d=d),
        grid=(m // tm, nj),
        in_specs=[pl.BlockSpec((tm, pw), lambda i, j: (i, 0)),
                  pl.BlockSpec((tm, sw), lambda i, j: (i, 0)),
                  pl.BlockSpec((tm, bw), lambda i, j: (i, 0)),
                  pl.BlockSpec((k, tn), lambda i, j: (0, j)),
                  pl.BlockSpec((tm, tn), lambda i, j: (i, j)),
                  pl.BlockSpec((1, tn), lambda i, j: (0, j))],
        out_specs=[pl.BlockSpec((tm, tn), lambda i, j: (i, j)),
                   pl.BlockSpec((tm, tn), lambda i, j: (i, j)),
                   pl.BlockSpec((tm, 1), lambda i, j: (i, 0))],
        out_shape=[jax.ShapeDtypeStruct((m, d), F32), jax.ShapeDtypeStruct((m, d), BF16),
                   jax.ShapeDtypeStruct((m, 1), F32)],
        scratch_shapes=[pltpu.VMEM((tm, 1), F32)],
        compiler_params=_params(("parallel", "arbitrary"), blocks,
                                _nbytes((tm, LANES), F32) + 3 * _nbytes((tm, tn), F32)),
        name="out_proj",
    )(y_pool, y_sgu, y_sb, w_out, h, g_next)


class _CastJobs:
    def __init__(self, jobs, grid):
        self.jobs = jobs
        self.plans = [self.plan(w.shape[1:], grid) for w, _ in jobs]
        assert all(p is not None for p in self.plans)

    @staticmethod
    def plan(shape, grid):
        rows, cols = shape
        for rows_on_first in (True, False):
            gr, gc = grid if rows_on_first else grid[::-1]
            if rows % gr == 0 and cols % gc == 0:
                block = (rows // gr, cols // gc)
                if block[0] % BF16_SUBLANES == 0 and block[1] % LANES == 0:
                    return block, rows_on_first
        return None

    def __len__(self):
        return len(self.jobs)

    def arrays(self):
        return [w for w, _ in self.jobs]

    def in_specs(self):
        return [pl.BlockSpec((None,) + block,
                             (lambda i, j, layer=layer: (layer, i, j)) if first else
                             (lambda i, j, layer=layer: (layer, j, i)))
                for (_, layer), (block, first) in zip(self.jobs, self.plans)]

    def out_specs(self):
        return [pl.BlockSpec(block, (lambda i, j: (i, j)) if first else (lambda i, j: (j, i)))
                for block, first in self.plans]

    def out_shapes(self):
        return [jax.ShapeDtypeStruct(w.shape[1:], BF16) for w, _ in self.jobs]

    def block_bytes(self):
        return sum(_nbytes(block, F32) + _nbytes(block, BF16) for block, _ in self.plans)


def _cast_kernel(w_ref, o_ref):
    o_ref[...] = w_ref[...].astype(o_ref.dtype)


def _cast_layer(w, layer):
    _, rows, cols = w.shape
    tr = _tile(rows, 128, BF16_SUBLANES)
    return pl.pallas_call(
        _cast_kernel,
        grid=(rows // tr,),
        in_specs=[pl.BlockSpec((None, tr, cols), lambda i: (layer, i, 0))],
        out_specs=pl.BlockSpec((tr, cols), lambda i: (i, 0)),
        out_shape=jax.ShapeDtypeStruct((rows, cols), BF16),
        compiler_params=_params(("parallel",), _nbytes((tr, cols), F32) + _nbytes((tr, cols), BF16)),
        name="cast_layer",
    )(w)


def _run_casts(refs, n):
    for src, dst in zip(refs[:n], refs[len(refs) - n:]):
        dst[...] = src[...].astype(dst.dtype)


def _gate_up_kernel(x_ref, rs_ref, wg_ref, wu_ref, *refs, chunks, n_casts):
    o_ref = refs[n_casts]
    rows_per_chunk = x_ref.shape[0] // chunks
    for c in range(chunks):
        rows = slice(c * rows_per_chunk, (c + 1) * rows_per_chunk)
        x = x_ref[rows, :]
        rs = rs_ref[rows, :]
        g = jnp.dot(x, wg_ref[...], preferred_element_type=F32) * rs
        u = jnp.dot(x, wu_ref[...], preferred_element_type=F32) * rs
        o_ref[rows, :] = (g * jax.nn.sigmoid(g) * u).astype(o_ref.dtype)
    _run_casts(refs, n_casts)


def _gate_up_grid(m, f):
    tm = _tile(m, 2048, 8)
    tf = _tile(f, 512, LANES)
    return tm, tf, (m // tm, f // tf)


def _gate_up(hg, rs, w_gate, w_up, cast_jobs):
    m, k = hg.shape
    f = w_gate.shape[1]
    tm, tf, grid = _gate_up_grid(m, f)
    casts = _CastJobs(cast_jobs, grid)
    blocks = (_nbytes((tm, k), BF16) + _nbytes((tm, LANES), F32) + 2 * _nbytes((k, tf), BF16)
              + _nbytes((tm, tf), BF16) + casts.block_bytes())
    return pl.pallas_call(
        functools.partial(_gate_up_kernel, chunks=2 if tm % 1024 == 0 else 1, n_casts=len(casts)),
        grid=grid,
        in_specs=[pl.BlockSpec((tm, k), lambda i, j: (i, 0)),
                  pl.BlockSpec((tm, 1), lambda i, j: (i, 0)),
                  pl.BlockSpec((k, tf), lambda i, j: (0, j)),
                  pl.BlockSpec((k, tf), lambda i, j: (0, j))] + casts.in_specs(),
        out_specs=[pl.BlockSpec((tm, tf), lambda i, j: (i, j))] + casts.out_specs(),
        out_shape=[jax.ShapeDtypeStruct((m, f), BF16)] + casts.out_shapes(),
        compiler_params=_params(("parallel", "arbitrary"), blocks, 3 * _nbytes((tm, tf), F32)),
        name="ffn_gate_up",
    )(hg, rs, w_gate, w_up, *casts.arrays())


def _down_kernel(a_ref, w_ref, h_ref, *refs, n_casts):
    o_ref = refs[n_casts]
    o_ref[...] = h_ref[...] + jnp.dot(a_ref[...], w_ref[...], preferred_element_type=F32)
    _run_casts(refs, n_casts)


def _down_grid(m, d):
    tm = _tile(m, 256, 8)
    tn = _tile(d, 1024, LANES)
    return tm, tn, (d // tn, m // tm)


def _down_proj(h, act, w_down, cast_jobs):
    m, d = h.shape
    f = act.shape[1]
    tm, tn, grid = _down_grid(m, d)
    casts = _CastJobs(cast_jobs, grid)
    blocks = _nbytes((tm, f), BF16) + 2 * _nbytes((tm, tn), F32) + casts.block_bytes()
    return pl.pallas_call(
        functools.partial(_down_kernel, n_casts=len(casts)),
        grid=grid,
        in_specs=[pl.BlockSpec((tm, f), lambda j, i: (i, 0)),
                  pl.BlockSpec((f, tn), lambda j, i: (0, j), pipeline_mode=pl.Buffered(1)),
                  pl.BlockSpec((tm, tn), lambda j, i: (i, j))] + casts.in_specs(),
        out_specs=[pl.BlockSpec((tm, tn), lambda j, i: (i, j))] + casts.out_specs(),
        out_shape=[jax.ShapeDtypeStruct((m, d), F32)] + casts.out_shapes(),
        compiler_params=_params(("arbitrary", "arbitrary"), blocks,
                                _nbytes((f, tn), BF16) + _nbytes((tm, tn), F32)),
        name="ffn_down",
    )(act, w_down, h, *casts.arrays())


def _ple_kernel(h_ref, p_ref, gn_ref, gd_ref, gu_ref, pp_ref, gnext_ref, *out_refs):
    h = h_ref[...]
    gn = (h * _rms_scale(h) * gn_ref[...]).astype(BF16)
    low = jnp.dot(gn, gd_ref[...], preferred_element_type=F32).astype(BF16)
    gate = jax.nn.sigmoid(jnp.dot(low, gu_ref[...], preferred_element_type=F32))
    emb = jnp.dot(p_ref[...].astype(BF16), pp_ref[...], preferred_element_type=F32)
    hn = h + gate * emb
    xn_ref = out_refs[-1]
    if len(out_refs) == 2:
        out_refs[0][...] = hn
    xn_ref[...] = (hn * _rms_scale(hn) * gnext_ref[...]).astype(xn_ref.dtype)


def _ple(h, p, g_ple, gate_down, gate_up, proj, g_next, last):
    m, d = h.shape
    e = p.shape[1]
    tm = _tile(m, 256, 8)
    row = pl.BlockSpec((tm, d), lambda i: (i, 0))
    vec = pl.BlockSpec((1, d), lambda i: (0, 0))
    if last:
        out_specs = [row]
        out_shape = [jax.ShapeDtypeStruct((m, d), F32)]
        out_bytes = _nbytes((tm, d), F32)
    else:
        out_specs = [row, row]
        out_shape = [jax.ShapeDtypeStruct((m, d), F32), jax.ShapeDtypeStruct((m, d), BF16)]
        out_bytes = _nbytes((tm, d), F32) + _nbytes((tm, d), BF16)
    blocks = (_nbytes((tm, d), F32) + _nbytes((tm, e), F32) + 2 * _nbytes((1, d), F32)
              + 3 * _nbytes((d, e), BF16) + out_bytes)
    return pl.pallas_call(
        _ple_kernel,
        grid=(m // tm,),
        in_specs=[row,
                  pl.BlockSpec((tm, e), lambda i: (i, 0)),
                  vec,
                  pl.BlockSpec((d, e), lambda i: (0, 0)),
                  pl.BlockSpec((e, d), lambda i: (0, 0)),
                  pl.BlockSpec((e, d), lambda i: (0, 0)),
                  vec],
        out_specs=out_specs,
        out_shape=out_shape,
        compiler_params=_params(("parallel",), blocks, 4 * _nbytes((tm, d), F32)),
        name="ple_gate",
    )(h, p, g_ple, gate_down, gate_up, proj, g_next)


def kernel(x, p, norm_mix_w, w_in, pool_w, pool_scale, sgu_norm_w, sgu_w, sgu_b, w_out, norm_ffn_w,
           w_gate, w_up, w_down, norm_ple_w, ple_gate_down, ple_gate_up, ple_proj, final_norm_w):
    batch, seq, d = x.shape
    depth = w_in.shape[0]
    m = batch * seq
    pool_width = pool_scale.shape[-1]
    sgu_width = sgu_norm_w.shape[-1]
    sb_width = w_out.shape[1] - pool_width - sgu_width
    n_heads = sb_width // HEAD_DIM
    rows_width = pool_width + 2 * sgu_width
    assert w_in.shape[2] == rows_width + 3 * sb_width and sb_width % HEAD_DIM == 0
    assert seq % LANES == 0 and sgu_w.shape[1] == sgu_width // HEAD_DIM
    assert pool_w.shape[1] == len(POOL_WINDOWS) and pool_w.shape[1] * pool_w.shape[2] == pool_width

    f = w_gate.shape[2]
    gate_up_grid, down_grid = _gate_up_grid(m, f)[2], _down_grid(m, d)[2]
    ride_gate_up = all(_CastJobs.plan(w.shape[1:], gate_up_grid) for w in (w_down, w_gate, w_up))
    ride_down = all(_CastJobs.plan(w.shape[1:], down_grid) for w in (w_in, w_out))

    w_in_b, w_out_b, w_gate_b, w_up_b = (_cast_layer(w, 0) for w in (w_in, w_out, w_gate, w_up))
    h = x.reshape(m, d)
    xn = _rmsnorm(h, norm_mix_w[0].reshape(1, d))
    out = None
    for i in range(depth):
        more = i + 1 < depth
        z_rows = _in_proj_rows(xn, w_in_b, 0, rows_width)
        z_heads = _in_proj_heads(xn, w_in_b, rows_width, 3 * sb_width, batch, seq)
        y_pool = _pool_mixer(z_rows, pool_w[i].astype(BF16), pool_scale[i].reshape(1, pool_width), seq)
        y_sgu = _sgu_mixer(z_rows, pool_width, sgu_norm_w[i].reshape(1, sgu_width), sgu_w[i],
                           sgu_b[i].T)
        y_sb = _sb_attention(z_heads, n_heads, batch, seq).reshape(m, sb_width)
        h, hg, rs = _out_proj(h, y_pool, y_sgu, y_sb, w_out_b, norm_ffn_w[i].reshape(1, d))

        jobs = [(w_down, i)] + ([(w_gate, i + 1), (w_up, i + 1)] if more else [])
        act, *cast = _gate_up(hg, rs, w_gate_b, w_up_b, jobs if ride_gate_up else [])
        if ride_gate_up:
            w_down_b = cast[0]
            if more:
                w_gate_b, w_up_b = cast[1], cast[2]
        else:
            w_down_b = _cast_layer(w_down, i)
            if more:
                w_gate_b, w_up_b = _cast_layer(w_gate, i + 1), _cast_layer(w_up, i + 1)

        jobs = [(w_in, i + 1), (w_out, i + 1)] if more and ride_down else []
        h, *cast = _down_proj(h, act, w_down_b, jobs)
        if jobs:
            w_in_b, w_out_b = cast
        elif more:
            w_in_b, w_out_b = _cast_layer(w_in, i + 1), _cast_layer(w_out, i + 1)
        last = i == depth - 1
        g_next = final_norm_w if last else norm_mix_w[i + 1]
        res = _ple(h, p[i].reshape(m, -1), norm_ple_w[i].reshape(1, d), ple_gate_down[i].astype(BF16),
                   ple_gate_up[i].astype(BF16), ple_proj[i].astype(BF16), g_next.reshape(1, d), last)
        if last:
            out = res[0]
        else:
            h, xn = res
    return out.reshape(batch, seq, d)
```

```python
import functools
import math

import jax
import jax.numpy as jnp
from jax import lax
from jax.experimental import pallas as pl
from jax.experimental.pallas import tpu as pltpu

F32 = jnp.float32
BF16 = jnp.bfloat16

RMS_EPS = 1e-6
LN_EPS = 1e-5
POOL_WINDOWS = (2, 4, 8, 16)
POOL_HALO = 16
HEAD_DIM = 128
LANES = 128
BF16_SUBLANES = 16
MIB = 1 << 20
V7X_VMEM_BYTES = 64 * MIB
VMEM_LIMIT_CAP = V7X_VMEM_BYTES - 4 * MIB
INTERNAL_SCRATCH_BYTES = 6 * MIB
LOG2E = math.log2(math.e)
UNDERFLOW_BITS = 150.0
OUT_OF_KEYS_CARRY = 1e30
Q_ROWS = 64
LEAD_BACK = 192


def _tile(n, target, align):
    if n <= target:
        return n
    t = (target // align) * align
    while t >= align:
        if n % t == 0:
            return t
        t -= align
    raise ValueError(f"no tile for n={n} target={target} align={align}")


def _nbytes(shape, dtype):
    return math.prod(shape) * jnp.dtype(dtype).itemsize


def _params(semantics, pipelined_bytes, resident_bytes=0):
    need = 2 * pipelined_bytes + resident_bytes + INTERNAL_SCRATCH_BYTES
    return pltpu.CompilerParams(
        dimension_semantics=semantics,
        vmem_limit_bytes=int(min(VMEM_LIMIT_CAP, max(need, 16 * MIB))),
    )


def _rms_scale(x):
    return lax.rsqrt(jnp.mean(x * x, axis=-1, keepdims=True) + RMS_EPS)


def _rmsnorm_kernel(x_ref, g_ref, o_ref):
    x = x_ref[...]
    o_ref[...] = (x * _rms_scale(x) * g_ref[...]).astype(o_ref.dtype)


def _rmsnorm(x, g):
    m, d = x.shape
    tm = _tile(m, 512, 8)
    return pl.pallas_call(
        _rmsnorm_kernel,
        grid=(m // tm,),
        in_specs=[pl.BlockSpec((tm, d), lambda i: (i, 0)), pl.BlockSpec((1, d), lambda i: (0, 0))],
        out_specs=pl.BlockSpec((tm, d), lambda i: (i, 0)),
        out_shape=jax.ShapeDtypeStruct((m, d), BF16),
        compiler_params=_params(("parallel",), _nbytes((tm, d), F32) + _nbytes((tm, d), BF16),
                                _nbytes((tm, d), F32)),
        name="rmsnorm",
    )(x, g)


def _matmul_kernel(a_ref, w_ref, o_ref):
    o_ref[...] = jnp.dot(a_ref[...], w_ref[...], preferred_element_type=F32).astype(o_ref.dtype)


def _in_proj_rows(xn, w, col0, ncols):
    m, k = xn.shape
    tm = _tile(m, 1024, 8)
    tn = _tile(math.gcd(ncols, col0) if col0 else ncols, 1024, LANES)
    off = col0 // tn
    blocks = _nbytes((tm, k), BF16) + _nbytes((k, tn), BF16) + _nbytes((tm, tn), BF16)
    return pl.pallas_call(
        _matmul_kernel,
        grid=(m // tm, ncols // tn),
        in_specs=[pl.BlockSpec((tm, k), lambda i, j: (i, 0)),
                  pl.BlockSpec((k, tn), lambda i, j: (0, j + off))],
        out_specs=pl.BlockSpec((tm, tn), lambda i, j: (i, j)),
        out_shape=jax.ShapeDtypeStruct((m, ncols), BF16),
        compiler_params=_params(("parallel", "arbitrary"), blocks, _nbytes((tm, tn), F32)),
        name="in_proj_rows",
    )(xn, w)


def _matmul_heads_kernel(a_ref, w_ref, o_ref, *, heads_per_tile):
    res = jnp.dot(a_ref[...], w_ref[...], preferred_element_type=F32)
    for hh in range(heads_per_tile):
        o_ref[hh] = res[:, hh * HEAD_DIM:(hh + 1) * HEAD_DIM].astype(o_ref.dtype)


def _in_proj_heads(xn, w, col0, ncols, batch, seq):
    m, k = xn.shape
    tm = _tile(seq, 1024, 8)
    tn = _tile(math.gcd(ncols, col0), 1024, LANES)
    off = col0 // tn
    hpt = tn // HEAD_DIM
    spt = seq // tm
    blocks = _nbytes((tm, k), BF16) + _nbytes((k, tn), BF16) + _nbytes((tm, tn), BF16)
    return pl.pallas_call(
        functools.partial(_matmul_heads_kernel, heads_per_tile=hpt),
        grid=(m // tm, ncols // tn),
        in_specs=[pl.BlockSpec((tm, k), lambda i, j: (i, 0)),
                  pl.BlockSpec((k, tn), lambda i, j: (0, j + off))],
        out_specs=pl.BlockSpec((None, hpt, tm, HEAD_DIM), lambda i, j: (i // spt, j, i % spt, 0)),
        out_shape=jax.ShapeDtypeStruct((batch, ncols // HEAD_DIM, seq, HEAD_DIM), BF16),
        compiler_params=_params(("parallel", "arbitrary"), blocks, _nbytes((tm, tn), F32)),
        name="in_proj_heads",
    )(xn, w)


def _pool_kernel(a_ref, halo_ref, pw_ref, ps_ref, o_ref, ext_ref, *, seq, tm, group):
    t0 = lax.rem(pl.program_id(0) * tm, seq)
    a = a_ref[...].astype(F32)
    ext_ref[0:POOL_HALO, :] = jnp.where(t0 == 0, 0.0, halo_ref[...].astype(F32))
    ext_ref[POOL_HALO:, :] = a
    pos = t0 + lax.broadcasted_iota(jnp.int32, (tm, 1), 0)
    for g, w in enumerate(POOL_WINDOWS):
        cols = slice(g * group, (g + 1) * group)
        win = a[:, cols]
        for back in range(1, w):
            win = win + ext_ref[POOL_HALO - back:POOL_HALO - back + tm, cols]
        cnt = jnp.minimum(pos + 1, w).astype(F32)
        pooled = win / cnt - a[:, cols]
        y = jnp.dot(pooled.astype(BF16), pw_ref[g], preferred_element_type=F32)
        o_ref[:, cols] = (y * ps_ref[:, cols]).astype(o_ref.dtype)


def _pool_mixer(z_rows, pool_w, pool_scale, seq):
    m = z_rows.shape[0]
    n_groups, group, _ = pool_w.shape
    width = n_groups * group
    tm = _tile(seq, 512, POOL_HALO)
    hpt = tm // POOL_HALO
    blocks = (_nbytes((tm, width), BF16) * 2 + _nbytes((POOL_HALO, width), BF16)
              + _nbytes(pool_w.shape, BF16) + _nbytes((1, width), F32))
    return pl.pallas_call(
        functools.partial(_pool_kernel, seq=seq, tm=tm, group=group),
        grid=(m // tm,),
        in_specs=[pl.BlockSpec((tm, width), lambda i: (i, 0)),
                  pl.BlockSpec((POOL_HALO, width), lambda i: (jnp.maximum(i * hpt - 1, 0), 0)),
                  pl.BlockSpec(pool_w.shape, lambda i: (0, 0, 0)),
                  pl.BlockSpec((1, width), lambda i: (0, 0))],
        out_specs=pl.BlockSpec((tm, width), lambda i: (i, 0)),
        out_shape=jax.ShapeDtypeStruct((m, width), BF16),
        scratch_shapes=[pltpu.VMEM((tm + POOL_HALO, width), F32)],
        compiler_params=_params(("parallel",), blocks, 4 * _nbytes((tm + POOL_HALO, width), F32)),
        name="pool_mixer",
    )(z_rows, z_rows, pool_w, pool_scale)


def _gelu_tanh(x):
    c = math.sqrt(2.0 / math.pi)
    half = 0.5 * x
    return half + half * jnp.tanh(x * (c + (c * 0.044715) * (x * x)))


def _sgu_kernel(u_ref, v_ref, nw_ref, ws_ref, b_ref, o_ref, *, heads, chunks):
    v = _gelu_tanh(v_ref[...].astype(F32))
    vc = v - jnp.mean(v, axis=-1, keepdims=True)
    var = jnp.mean(vc * vc, axis=-1, keepdims=True)
    vn = (vc * lax.rsqrt(var + LN_EPS) * nw_ref[...]).astype(BF16)
    row = lax.broadcasted_iota(jnp.int32, (HEAD_DIM, HEAD_DIM), 0)
    col = lax.broadcasted_iota(jnp.int32, (HEAD_DIM, HEAD_DIM), 1)
    causal = row >= col
    for h in range(heads):
        cols = slice(h * HEAD_DIM, (h + 1) * HEAD_DIM)
        w = jnp.where(causal, ws_ref[h], 0.0).astype(BF16)
        bias = b_ref[:, h:h + 1]
        for c in range(chunks):
            rows = slice(c * HEAD_DIM, (c + 1) * HEAD_DIM)
            mixed = jnp.dot(w, vn[rows, cols], preferred_element_type=F32) + bias
            u = _gelu_tanh(u_ref[rows, cols].astype(F32))
            o_ref[rows, cols] = (u * mixed).astype(o_ref.dtype)


def _sgu_mixer(z_rows, col0, norm_w, w_s, b_t):
    m = z_rows.shape[0]
    heads = w_s.shape[0]
    width = heads * HEAD_DIM
    assert col0 % width == 0 and w_s.shape[1:] == (HEAD_DIM, HEAD_DIM)
    cb = col0 // width
    tm = _tile(m, 256, HEAD_DIM)
    blocks = (3 * _nbytes((tm, width), BF16) + _nbytes((1, width), F32) + _nbytes(w_s.shape, F32)
              + _nbytes((HEAD_DIM, LANES), F32))
    return pl.pallas_call(
        functools.partial(_sgu_kernel, heads=heads, chunks=tm // HEAD_DIM),
        grid=(m // tm,),
        in_specs=[pl.BlockSpec((tm, width), lambda i: (i, cb)),
                  pl.BlockSpec((tm, width), lambda i: (i, cb + 1)),
                  pl.BlockSpec((1, width), lambda i: (0, 0)),
                  pl.BlockSpec(w_s.shape, lambda i: (0, 0, 0)),
                  pl.BlockSpec(b_t.shape, lambda i: (0, 0))],
        out_specs=pl.BlockSpec((tm, width), lambda i: (i, 0)),
        out_shape=jax.ShapeDtypeStruct((m, width), BF16),
        compiler_params=_params(("parallel",), blocks, 4 * _nbytes((tm, width), F32)),
        name="sgu_mixer",
    )(z_rows, z_rows, norm_w, w_s, b_t)


def _neg_abs(x):
    bits = lax.bitcast_convert_type(x, jnp.uint32) | jnp.uint32(0x80000000)
    return lax.bitcast_convert_type(bits, F32)


def _bf16_split(x):
    bits = lax.bitcast_convert_type(x, jnp.uint32) & jnp.uint32(0xFFFF0000)
    hi = lax.bitcast_convert_type(bits, F32)
    return hi, x - hi


def _sb_streams(streams, cumsum_rhs, c2):
    scores = [lax.dot_general(q, kwin, (((1,), (1,)), ((), ())), preferred_element_type=F32)
              for q, kwin, _, _, _ in streams]
    log_betas, sums = [], []
    for s, (_, _, _, _, masks) in zip(scores, streams):
        z2 = s * c2
        sp = jnp.maximum(z2, 0.0) + jnp.log(1.0 + jnp.exp2(_neg_abs(z2))) * LOG2E
        log_betas.append(z2 - sp)
        lhs = []
        for j, mask in enumerate(masks):
            spj = sp[:, j * LANES:(j + 1) * LANES]
            if mask is not None:
                spj = jnp.where(mask, spj, 0.0)
            hi, lo = _bf16_split(spj)
            lhs.append(jnp.concatenate([hi.astype(BF16), lo.astype(BF16)], axis=1))
        lhs = lhs[0] if len(lhs) == 1 else jnp.concatenate(lhs, axis=0)
        sums.append(jnp.dot(lhs, cumsum_rhs, preferred_element_type=F32))
    out = []
    for (q, _, vwin, carry, masks), log_beta, both in zip(streams, log_betas, sums):
        r = q.shape[0]
        weights = [None] * len(masks)
        for j in reversed(range(len(masks))):
            inner = both[j * r:(j + 1) * r, :LANES]
            total = both[j * r:(j + 1) * r, LANES:]
            suffix = inner if carry is None else inner + carry
            pj = jnp.exp2(log_beta[:, j * LANES:(j + 1) * LANES] - suffix)
            if masks[j] is not None:
                pj = jnp.where(masks[j], pj, 0.0)
            weights[j] = pj.astype(BF16)
            carry = total if carry is None else carry + total
        p = weights[0] if len(weights) == 1 else jnp.concatenate(weights, axis=1)
        out.append((jnp.dot(p, vwin, preferred_element_type=F32), carry))
    return out


def _attn_kernel(q_ref, k_ref, v_ref, o_ref, *, heads, n_blocks, peel, group, c2):
    r = Q_ROWS
    krow = lax.broadcasted_iota(jnp.int32, (LANES, LANES), 0)
    kcol = lax.broadcasted_iota(jnp.int32, (LANES, LANES), 1)
    half = jnp.concatenate([(krow > kcol).astype(BF16), jnp.ones((LANES, LANES), BF16)], axis=1)
    cumsum_rhs = jnp.concatenate([half, half], axis=0)
    col = lax.broadcasted_iota(jnp.int32, (r, LANES), 1)
    ahead = col - lax.broadcasted_iota(jnp.int32, (r, LANES), 0)
    run = functools.partial(_sb_streams, cumsum_rhs=cumsum_rhs, c2=c2)

    def span(start, size):
        if isinstance(start, int):
            return slice(start, start + size)
        return pl.ds(pl.multiple_of(start, r), size)

    def run_blocks(blocks):
        def keys_back(b):
            return min(b * r, LEAD_BACK) if isinstance(b, int) else LEAD_BACK

        tasks = [(hh, b, keys_back(b)) for hh in range(heads) for b in blocks]
        streams = []
        for hh, b, back in tasks:
            n = -(-(back + r) // LANES)
            masks = [None if back - j * LANES >= LANES else ahead < back - j * LANES for j in range(n)]
            streams.append((q_ref[hh, span(b * r, r), :], k_ref[hh, span(b * r - back, n * LANES), :],
                            v_ref[hh, span(b * r - back, n * LANES), :], None, masks))
        done = run(streams)
        accs = tuple(acc for acc, _ in done)
        if any(not isinstance(b, int) or b * r - back > 0 for _, b, back in tasks):
            def left_edge(b, back, step):
                return b * r - back - step * LANES

            def cond(st):
                step, carries, _ = st
                least = None
                for (_, b, back), carry in zip(tasks, carries):
                    carry = carry + jnp.where(left_edge(b, back, step) <= 0, OUT_OF_KEYS_CARRY, 0.0)
                    least = carry if least is None else jnp.minimum(least, carry)
                return jnp.min(least) < UNDERFLOW_BITS

            def body(st):
                step, carries, accs = st
                more = []
                for (hh, b, back), carry in zip(tasks, carries):
                    edge = left_edge(b, back, step)
                    start = jnp.maximum(edge - LANES, 0)
                    carry = carry + jnp.where(edge <= 0, OUT_OF_KEYS_CARRY, 0.0)
                    more.append((q_ref[hh, span(b * r, r), :], k_ref[hh, span(start, LANES), :],
                                 v_ref[hh, span(start, LANES), :], carry, [col < edge - start]))
                parts = run(more)
                return (step + 1, tuple(carry for _, carry in parts),
                        tuple(acc + part for acc, (part, _) in zip(accs, parts)))

            _, _, accs = lax.while_loop(cond, body, (0, tuple(carry for _, carry in done), accs))
        for (hh, b, _), acc in zip(tasks, accs):
            o_ref[span(b * r, r), hh * HEAD_DIM:(hh + 1) * HEAD_DIM] = acc.astype(o_ref.dtype)

    for first in range(0, peel, group):
        run_blocks(list(range(first, min(first + group, peel))))

    def group_body(g, _):
        run_blocks([peel + g * group + t for t in range(group)])
        return 0

    if n_blocks > peel:
        lax.fori_loop(0, (n_blocks - peel) // group, group_body, 0)


def _sb_attention(z_heads, n_heads, batch, seq):
    heads = math.gcd(n_heads, 4)
    n_blocks = seq // Q_ROWS
    min_peel = min(-(-LEAD_BACK // Q_ROWS), n_blocks)
    group = min(4, max(n_blocks - min_peel, 1))
    peel = min_peel + (n_blocks - min_peel) % group
    hb = n_heads // heads
    blocks = 4 * _nbytes((heads, seq, HEAD_DIM), BF16)
    kern = functools.partial(_attn_kernel, heads=heads, n_blocks=n_blocks, peel=peel, group=group,
                             c2=HEAD_DIM ** -0.5 * LOG2E)
    return pl.pallas_call(
        kern,
        grid=(batch, hb),
        in_specs=[pl.BlockSpec((None, heads, seq, HEAD_DIM), lambda b, h: (b, h, 0, 0)),
                  pl.BlockSpec((None, heads, seq, HEAD_DIM), lambda b, h: (b, hb + h, 0, 0)),
                  pl.BlockSpec((None, heads, seq, HEAD_DIM), lambda b, h: (b, 2 * hb + h, 0, 0))],
        out_specs=pl.BlockSpec((None, seq, heads * HEAD_DIM), lambda b, h: (b, 0, h)),
        out_shape=jax.ShapeDtypeStruct((batch, seq, n_heads * HEAD_DIM), BF16),
        compiler_params=_params(("parallel", "parallel"), blocks, 8 * MIB),
        name="sb_attention",
    )(z_heads, z_heads, z_heads)


def _out_proj_kernel(yp_ref, ys_ref, yb_ref, w_ref, h_ref, g_ref, ho_ref, hg_ref, rs_ref,
                     ssq_ref, *, pw, sw, nj, d):
    j = pl.program_id(1)

    @pl.when(j == 0)
    def _():
        ssq_ref[...] = jnp.zeros_like(ssq_ref)

    acc = jnp.dot(yp_ref[...], w_ref[0:pw, :], preferred_element_type=F32)
    acc += jnp.dot(ys_ref[...], w_ref[pw:pw + sw, :], preferred_element_type=F32)
    acc += jnp.dot(yb_ref[...], w_ref[pw + sw:, :], preferred_element_type=F32)
    hn = h_ref[...] + acc
    ho_ref[...] = hn
    hg_ref[...] = (hn * g_ref[...]).astype(hg_ref.dtype)
    ssq_ref[...] += jnp.sum(hn * hn, axis=-1, keepdims=True)

    @pl.when(j == nj - 1)
    def _():
        rs_ref[...] = lax.rsqrt(ssq_ref[...] * (1.0 / d) + RMS_EPS)


def _out_proj(h, y_pool, y_sgu, y_sb, w_out, g_next):
    m, d = h.shape
    pw, sw, bw = y_pool.shape[1], y_sgu.shape[1], y_sb.shape[1]
    k = pw + sw + bw
    tm = _tile(m, 1024, 8)
    tn = _tile(d, 512, LANES)
    nj = d // tn
    blocks = (_nbytes((tm, k), BF16) + _nbytes((k, tn), BF16) + 2 * _nbytes((tm, tn), F32)
              + _nbytes((8, tn), F32) + _nbytes((tm, tn), BF16) + _nbytes((tm, LANES), F32))
    return pl.pallas_call(
        functools.partial(_out_proj_kernel, pw=pw, sw=sw, nj=nj, d=d),
        grid=(m // tm, nj),
        in_specs=[pl.BlockSpec((tm, pw), lambda i, j: (i, 0)),
                  pl.BlockSpec((tm, sw), lambda i, j: (i, 0)),
                  pl.BlockSpec((tm, bw), lambda i, j: (i, 0)),
                  pl.BlockSpec((k, tn), lambda i, j: (0, j)),
                  pl.BlockSpec((tm, tn), lambda i, j: (i, j)),
                  pl.BlockSpec((1, tn), lambda i, j: (0, j))],
        out_specs=[pl.BlockSpec((tm, tn), lambda i, j: (i, j)),
                   pl.BlockSpec((tm, tn), lambda i, j: (i, j)),
                   pl.BlockSpec((tm, 1), lambda i, j: (i, 0))],
        out_shape=[jax.ShapeDtypeStruct((m, d), F32), jax.ShapeDtypeStruct((m, d), BF16),
                   jax.ShapeDtypeStruct((m, 1), F32)],
        scratch_shapes=[pltpu.VMEM((tm, 1), F32)],
        compiler_params=_params(("parallel", "arbitrary"), blocks,
                                _nbytes((tm, LANES), F32) + 3 * _nbytes((tm, tn), F32)),
        name="out_proj",
    )(y_pool, y_sgu, y_sb, w_out, h, g_next)


class _CastJobs:
    def __init__(self, jobs, grid):
        self.jobs = jobs
        self.plans = [self.plan(w.shape[1:], grid) for w, _ in jobs]
        assert all(p is not None for p in self.plans)

    @staticmethod
    def plan(shape, grid):
        rows, cols = shape
        for rows_on_first in (True, False):
            gr, gc = grid if rows_on_first else grid[::-1]
            if rows % gr == 0 and cols % gc == 0:
                block = (rows // gr, cols // gc)
                if block[0] % BF16_SUBLANES == 0 and block[1] % LANES == 0:
                    return block, rows_on_first
        return None

    def __len__(self):
        return len(self.jobs)

    def arrays(self):
        return [w for w, _ in self.jobs]

    def in_specs(self):
        return [pl.BlockSpec((None,) + block,
                             (lambda i, j, layer=layer: (layer, i, j)) if first else
                             (lambda i, j, layer=layer: (layer, j, i)))
                for (_, layer), (block, first) in zip(self.jobs, self.plans)]

    def out_specs(self):
        return [pl.BlockSpec(block, (lambda i, j: (i, j)) if first else (lambda i, j: (j, i)))
                for block, first in self.plans]

    def out_shapes(self):
        return [jax.ShapeDtypeStruct(w.shape[1:], BF16) for w, _ in self.jobs]

    def block_bytes(self):
        return sum(_nbytes(block, F32) + _nbytes(block, BF16) for block, _ in self.plans)


def _cast_kernel(w_ref, o_ref):
    o_ref[...] = w_ref[...].astype(o_ref.dtype)


def _cast_layer(w, layer):
    _, rows, cols = w.shape
    tr = _tile(rows, 128, BF16_SUBLANES)
    return pl.pallas_call(
        _cast_kernel,
        grid=(rows // tr,),
        in_specs=[pl.BlockSpec((None, tr, cols), lambda i: (layer, i, 0))],
        out_specs=pl.BlockSpec((tr, cols), lambda i: (i, 0)),
        out_shape=jax.ShapeDtypeStruct((rows, cols), BF16),
        compiler_params=_params(("parallel",), _nbytes((tr, cols), F32) + _nbytes((tr, cols), BF16)),
        name="cast_layer",
    )(w)


def _run_casts(refs, n):
    for src, dst in zip(refs[:n], refs[len(refs) - n:]):
        dst[...] = src[...].astype(dst.dtype)


def _gate_up_kernel(x_ref, rs_ref, wg_ref, wu_ref, *refs, chunks, n_casts):
    o_ref = refs[n_casts]
    rows_per_chunk = x_ref.shape[0] // chunks
    for c in range(chunks):
        rows = slice(c * rows_per_chunk, (c + 1) * rows_per_chunk)
        x = x_ref[rows, :]
        rs = rs_ref[rows, :]
        g = jnp.dot(x, wg_ref[...], preferred_element_type=F32) * rs
        u = jnp.dot(x, wu_ref[...], preferred_element_type=F32)
        o_ref[rows, :] = (g * jax.nn.sigmoid(g) * u).astype(o_ref.dtype)
    _run_casts(refs, n_casts)


def _gate_up_grid(m, f):
    tm = _tile(m, 2048, 8)
    tf = _tile(f, 512, LANES)
    return tm, tf, (m // tm, f // tf)


def _gate_up(hg, rs, w_gate, w_up, cast_jobs):
    m, k = hg.shape
    f = w_gate.shape[1]
    tm, tf, grid = _gate_up_grid(m, f)
    casts = _CastJobs(cast_jobs, grid)
    blocks = (_nbytes((tm, k), BF16) + _nbytes((tm, LANES), F32) + 2 * _nbytes((k, tf), BF16)
              + _nbytes((tm, tf), BF16) + casts.block_bytes())
    return pl.pallas_call(
        functools.partial(_gate_up_kernel, chunks=2 if tm % 1024 == 0 else 1, n_casts=len(casts)),
        grid=grid,
        in_specs=[pl.BlockSpec((tm, k), lambda i, j: (i, 0)),
                  pl.BlockSpec((tm, 1), lambda i, j: (i, 0)),
                  pl.BlockSpec((k, tf), lambda i, j: (0, j)),
                  pl.BlockSpec((k, tf), lambda i, j: (0, j))] + casts.in_specs(),
        out_specs=[pl.BlockSpec((tm, tf), lambda i, j: (i, j))] + casts.out_specs(),
        out_shape=[jax.ShapeDtypeStruct((m, f), BF16)] + casts.out_shapes(),
        compiler_params=_params(("parallel", "arbitrary"), blocks, 3 * _nbytes((tm, tf), F32)),
        name="ffn_gate_up",
    )(hg, rs, w_gate, w_up, *casts.arrays())


def _down_kernel(a_ref, w_ref, h_ref, rs_ref, *refs, n_casts):
    o_ref = refs[n_casts]
    o_ref[...] = h_ref[...] + rs_ref[...] * jnp.dot(a_ref[...], w_ref[...], preferred_element_type=F32)
    _run_casts(refs, n_casts)


def _down_grid(m, d):
    tm = _tile(m, 256, 8)
    tn = _tile(d, 1024, LANES)
    return tm, tn, (d // tn, m // tm)


def _down_proj(h, act, rs, w_down, cast_jobs):
    m, d = h.shape
    f = act.shape[1]
    tm, tn, grid = _down_grid(m, d)
    casts = _CastJobs(cast_jobs, grid)
    blocks = (_nbytes((tm, f), BF16) + 2 * _nbytes((tm, tn), F32) + _nbytes((tm, LANES), F32)
              + casts.block_bytes())
    return pl.pallas_call(
        functools.partial(_down_kernel, n_casts=len(casts)),
        grid=grid,
        in_specs=[pl.BlockSpec((tm, f), lambda j, i: (i, 0)),
                  pl.BlockSpec((f, tn), lambda j, i: (0, j), pipeline_mode=pl.Buffered(1)),
                  pl.BlockSpec((tm, tn), lambda j, i: (i, j)),
                  pl.BlockSpec((tm, 1), lambda j, i: (i, 0))] + casts.in_specs(),
        out_specs=[pl.BlockSpec((tm, tn), lambda j, i: (i, j))] + casts.out_specs(),
        out_shape=[jax.ShapeDtypeStruct((m, d), F32)] + casts.out_shapes(),
        compiler_params=_params(("arbitrary", "arbitrary"), blocks,
                                _nbytes((f, tn), BF16) + _nbytes((tm, tn), F32)),
        name="ffn_down",
    )(act, w_down, h, rs, *casts.arrays())


def _ple_kernel(h_ref, p_ref, gn_ref, gd_ref, gu_ref, pp_ref, gnext_ref, *out_refs):
    h = h_ref[...]
    hg = (h * gn_ref[...]).astype(BF16)
    low = (jnp.dot(hg, gd_ref[...], preferred_element_type=F32) * _rms_scale(h)).astype(BF16)
    gate = jax.nn.sigmoid(jnp.dot(low, gu_ref[...], preferred_element_type=F32))
    emb = jnp.dot(p_ref[...].astype(BF16), pp_ref[...], preferred_element_type=F32)
    hn = h + gate * emb
    xn_ref = out_refs[-1]
    if len(out_refs) == 2:
        out_refs[0][...] = hn
    xn_ref[...] = (hn * _rms_scale(hn) * gnext_ref[...]).astype(xn_ref.dtype)


def _ple(h, p, g_ple, gate_down, gate_up, proj, g_next, last):
    m, d = h.shape
    e = p.shape[1]
    tm = _tile(m, 256, 8)
    row = pl.BlockSpec((tm, d), lambda i: (i, 0))
    vec = pl.BlockSpec((1, d), lambda i: (0, 0))
    if last:
        out_specs = [row]
        out_shape = [jax.ShapeDtypeStruct((m, d), F32)]
        out_bytes = _nbytes((tm, d), F32)
    else:
        out_specs = [row, row]
        out_shape = [jax.ShapeDtypeStruct((m, d), F32), jax.ShapeDtypeStruct((m, d), BF16)]
        out_bytes = _nbytes((tm, d), F32) + _nbytes((tm, d), BF16)
    blocks = (_nbytes((tm, d), F32) + _nbytes((tm, e), F32) + 2 * _nbytes((1, d), F32)
              + 3 * _nbytes((d, e), BF16) + out_bytes)
    return pl.pallas_call(
        _ple_kernel,
        grid=(m // tm,),
        in_specs=[row,
                  pl.BlockSpec((tm, e), lambda i: (i, 0)),
                  vec,
                  pl.BlockSpec((d, e), lambda i: (0, 0)),
                  pl.BlockSpec((e, d), lambda i: (0, 0)),
                  pl.BlockSpec((e, d), lambda i: (0, 0)),
                  vec],
        out_specs=out_specs,
        out_shape=out_shape,
        compiler_params=_params(("parallel",), blocks, 4 * _nbytes((tm, d), F32)),
        name="ple_gate",
    )(h, p, g_ple, gate_down, gate_up, proj, g_next)


def kernel(x, p, norm_mix_w, w_in, pool_w, pool_scale, sgu_norm_w, sgu_w, sgu_b, w_out, norm_ffn_w,
           w_gate, w_up, w_down, norm_ple_w, ple_gate_down, ple_gate_up, ple_proj, final_norm_w):
    batch, seq, d = x.shape
    depth = w_in.shape[0]
    m = batch * seq
    pool_width = pool_scale.shape[-1]
    sgu_width = sgu_norm_w.shape[-1]
    sb_width = w_out.shape[1] - pool_width - sgu_width
    n_heads = sb_width // HEAD_DIM
    rows_width = pool_width + 2 * sgu_width
    assert w_in.shape[2] == rows_width + 3 * sb_width and sb_width % HEAD_DIM == 0
    assert seq % LANES == 0 and sgu_w.shape[1] == sgu_width // HEAD_DIM
    assert pool_w.shape[1] == len(POOL_WINDOWS) and pool_w.shape[1] * pool_w.shape[2] == pool_width

    f = w_gate.shape[2]
    gate_up_grid, down_grid = _gate_up_grid(m, f)[2], _down_grid(m, d)[2]
    ride_gate_up = all(_CastJobs.plan(w.shape[1:], gate_up_grid) for w in (w_down, w_gate, w_up))
    ride_down = all(_CastJobs.plan(w.shape[1:], down_grid) for w in (w_in, w_out))

    w_in_b, w_out_b, w_gate_b, w_up_b = (_cast_layer(w, 0) for w in (w_in, w_out, w_gate, w_up))
    h = x.reshape(m, d)
    xn = _rmsnorm(h, norm_mix_w[0].reshape(1, d))
    out = None
    for i in range(depth):
        more = i + 1 < depth
        z_rows = _in_proj_rows(xn, w_in_b, 0, rows_width)
        z_heads = _in_proj_heads(xn, w_in_b, rows_width, 3 * sb_width, batch, seq)
        y_pool = _pool_mixer(z_rows, pool_w[i].astype(BF16), pool_scale[i].reshape(1, pool_width), seq)
        y_sgu = _sgu_mixer(z_rows, pool_width, sgu_norm_w[i].reshape(1, sgu_width), sgu_w[i],
                           sgu_b[i].T)
        y_sb = _sb_attention(z_heads, n_heads, batch, seq).reshape(m, sb_width)
        h, hg, rs = _out_proj(h, y_pool, y_sgu, y_sb, w_out_b, norm_ffn_w[i].reshape(1, d))

        jobs = [(w_down, i)] + ([(w_gate, i + 1), (w_up, i + 1)] if more else [])
        act, *cast = _gate_up(hg, rs, w_gate_b, w_up_b, jobs if ride_gate_up else [])
        if ride_gate_up:
            w_down_b = cast[0]
            if more:
                w_gate_b, w_up_b = cast[1], cast[2]
        else:
            w_down_b = _cast_layer(w_down, i)
            if more:
                w_gate_b, w_up_b = _cast_layer(w_gate, i + 1), _cast_layer(w_up, i + 1)

        jobs = [(w_in, i + 1), (w_out, i + 1)] if more and ride_down else []
        h, *cast = _down_proj(h, act, rs, w_down_b, jobs)
        if jobs:
            w_in_b, w_out_b = cast
        elif more:
            w_in_b, w_out_b = _cast_layer(w_in, i + 1), _cast_layer(w_out, i + 1)
        last = i == depth - 1
        g_next = final_norm_w if last else norm_mix_w[i + 1]
        res = _ple(h, p[i].reshape(m, -1), norm_ple_w[i].reshape(1, d), ple_gate_down[i].astype(BF16),
                   ple_gate_up[i].astype(BF16), ple_proj[i].astype(BF16), g_next.reshape(1, d), last)
        if last:
            out = res[0]
        else:
            h, xn = res
    return out.reshape(batch, seq, d)
```

```python
import functools
import math

import jax
import jax.numpy as jnp
from jax import lax
from jax.experimental import pallas as pl
from jax.experimental.pallas import tpu as pltpu

F32 = jnp.float32
BF16 = jnp.bfloat16

RMS_EPS = 1e-6
LN_EPS = 1e-5
POOL_WINDOWS = (2, 4, 8, 16)
POOL_HALO = 16
HEAD_DIM = 128
LANES = 128
BF16_SUBLANES = 16
MIB = 1 << 20
V7X_VMEM_BYTES = 64 * MIB
VMEM_LIMIT_CAP = V7X_VMEM_BYTES - 4 * MIB
INTERNAL_SCRATCH_BYTES = 6 * MIB
CAST_BLOCK_MAX_BYTES = 2 * MIB
LOG2E = math.log2(math.e)
UNDERFLOW_BITS = 150.0
OUT_OF_KEYS_CARRY = 1e30
Q_ROWS = 64
LEAD_BACK = 192


def _tile(n, target, align):
    if n <= target:
        return n
    t = (target // align) * align
    while t >= align:
        if n % t == 0:
            return t
        t -= align
    raise ValueError(f"no tile for n={n} target={target} align={align}")


def _nbytes(shape, dtype):
    return math.prod(shape) * jnp.dtype(dtype).itemsize


def _params(semantics, pipelined_bytes, resident_bytes=0):
    need = 2 * pipelined_bytes + resident_bytes + INTERNAL_SCRATCH_BYTES
    assert need <= V7X_VMEM_BYTES, (need, V7X_VMEM_BYTES)
    return pltpu.CompilerParams(dimension_semantics=semantics, vmem_limit_bytes=VMEM_LIMIT_CAP)


def _rms_scale(x):
    return lax.rsqrt(jnp.mean(x * x, axis=-1, keepdims=True) + RMS_EPS)


def _rmsnorm_kernel(x_ref, g_ref, o_ref):
    x = x_ref[...]
    o_ref[...] = (x * _rms_scale(x) * g_ref[...]).astype(o_ref.dtype)


def _rmsnorm(x, g):
    m, d = x.shape
    tm = _tile(m, 512, 8)
    return pl.pallas_call(
        _rmsnorm_kernel,
        grid=(m // tm,),
        in_specs=[pl.BlockSpec((tm, d), lambda i: (i, 0)), pl.BlockSpec((1, d), lambda i: (0, 0))],
        out_specs=pl.BlockSpec((tm, d), lambda i: (i, 0)),
        out_shape=jax.ShapeDtypeStruct((m, d), BF16),
        compiler_params=_params(("parallel",), _nbytes((tm, d), F32) + _nbytes((tm, d), BF16),
                                _nbytes((tm, d), F32)),
        name="rmsnorm",
    )(x, g)


def _matmul_kernel(a_ref, w_ref, o_ref):
    o_ref[...] = jnp.dot(a_ref[...], w_ref[...], preferred_element_type=F32).astype(o_ref.dtype)


def _in_proj_rows(xn, w, col0, ncols):
    m, k = xn.shape
    tm = _tile(m, 1024, 8)
    tn = _tile(math.gcd(ncols, col0) if col0 else ncols, 1024, LANES)
    off = col0 // tn
    blocks = _nbytes((tm, k), BF16) + _nbytes((k, tn), BF16) + _nbytes((tm, tn), BF16)
    return pl.pallas_call(
        _matmul_kernel,
        grid=(m // tm, ncols // tn),
        in_specs=[pl.BlockSpec((tm, k), lambda i, j: (i, 0)),
                  pl.BlockSpec((k, tn), lambda i, j: (0, j + off))],
        out_specs=pl.BlockSpec((tm, tn), lambda i, j: (i, j)),
        out_shape=jax.ShapeDtypeStruct((m, ncols), BF16),
        compiler_params=_params(("parallel", "arbitrary"), blocks, _nbytes((tm, tn), F32)),
        name="in_proj_rows",
    )(xn, w)


def _matmul_heads_kernel(a_ref, w_ref, o_ref, *, heads_per_tile):
    res = jnp.dot(a_ref[...], w_ref[...], preferred_element_type=F32)
    for hh in range(heads_per_tile):
        o_ref[hh] = res[:, hh * HEAD_DIM:(hh + 1) * HEAD_DIM].astype(o_ref.dtype)


def _in_proj_heads(xn, w, col0, ncols, batch, seq):
    m, k = xn.shape
    tm = _tile(seq, 1024, 8)
    tn = _tile(math.gcd(ncols, col0), 1024, LANES)
    off = col0 // tn
    hpt = tn // HEAD_DIM
    spt = seq // tm
    blocks = _nbytes((tm, k), BF16) + _nbytes((k, tn), BF16) + _nbytes((tm, tn), BF16)
    return pl.pallas_call(
        functools.partial(_matmul_heads_kernel, heads_per_tile=hpt),
        grid=(m // tm, ncols // tn),
        in_specs=[pl.BlockSpec((tm, k), lambda i, j: (i, 0)),
                  pl.BlockSpec((k, tn), lambda i, j: (0, j + off))],
        out_specs=pl.BlockSpec((None, hpt, tm, HEAD_DIM), lambda i, j: (i // spt, j, i % spt, 0)),
        out_shape=jax.ShapeDtypeStruct((batch, ncols // HEAD_DIM, seq, HEAD_DIM), BF16),
        compiler_params=_params(("parallel", "arbitrary"), blocks, _nbytes((tm, tn), F32)),
        name="in_proj_heads",
    )(xn, w)


def _pool_kernel(a_ref, halo_ref, pw_ref, ps_ref, o_ref, ext_ref, *, seq, tm, group):
    t0 = lax.rem(pl.program_id(0) * tm, seq)
    a = a_ref[...].astype(F32)
    ext_ref[0:POOL_HALO, :] = jnp.where(t0 == 0, 0.0, halo_ref[...].astype(F32))
    ext_ref[POOL_HALO:, :] = a
    pos = t0 + lax.broadcasted_iota(jnp.int32, (tm, 1), 0)
    for g, w in enumerate(POOL_WINDOWS):
        cols = slice(g * group, (g + 1) * group)
        win = a[:, cols]
        for back in range(1, w):
            win = win + ext_ref[POOL_HALO - back:POOL_HALO - back + tm, cols]
        cnt = jnp.minimum(pos + 1, w).astype(F32)
        pooled = win / cnt - a[:, cols]
        y = jnp.dot(pooled.astype(BF16), pw_ref[g], preferred_element_type=F32)
        o_ref[:, cols] = (y * ps_ref[:, cols]).astype(o_ref.dtype)


def _pool_mixer(z_rows, pool_w, pool_scale, seq):
    m = z_rows.shape[0]
    n_groups, group, _ = pool_w.shape
    width = n_groups * group
    tm = _tile(seq, 512, POOL_HALO)
    hpt = tm // POOL_HALO
    blocks = (_nbytes((tm, width), BF16) * 2 + _nbytes((POOL_HALO, width), BF16)
              + _nbytes(pool_w.shape, BF16) + _nbytes((1, width), F32))
    return pl.pallas_call(
        functools.partial(_pool_kernel, seq=seq, tm=tm, group=group),
        grid=(m // tm,),
        in_specs=[pl.BlockSpec((tm, width), lambda i: (i, 0)),
                  pl.BlockSpec((POOL_HALO, width), lambda i: (jnp.maximum(i * hpt - 1, 0), 0)),
                  pl.BlockSpec(pool_w.shape, lambda i: (0, 0, 0)),
                  pl.BlockSpec((1, width), lambda i: (0, 0))],
        out_specs=pl.BlockSpec((tm, width), lambda i: (i, 0)),
        out_shape=jax.ShapeDtypeStruct((m, width), BF16),
        scratch_shapes=[pltpu.VMEM((tm + POOL_HALO, width), F32)],
        compiler_params=_params(("parallel",), blocks, 4 * _nbytes((tm + POOL_HALO, width), F32)),
        name="pool_mixer",
    )(z_rows, z_rows, pool_w, pool_scale)


def _gelu_tanh(x):
    c = math.sqrt(2.0 / math.pi)
    half = 0.5 * x
    return half + half * jnp.tanh(x * (c + (c * 0.044715) * (x * x)))


def _sgu_kernel(u_ref, v_ref, nw_ref, ws_ref, b_ref, o_ref, *, heads, chunks):
    v = _gelu_tanh(v_ref[...].astype(F32))
    vc = v - jnp.mean(v, axis=-1, keepdims=True)
    var = jnp.mean(vc * vc, axis=-1, keepdims=True)
    vn = (vc * lax.rsqrt(var + LN_EPS) * nw_ref[...]).astype(BF16)
    row = lax.broadcasted_iota(jnp.int32, (HEAD_DIM, HEAD_DIM), 0)
    col = lax.broadcasted_iota(jnp.int32, (HEAD_DIM, HEAD_DIM), 1)
    causal = row >= col
    for h in range(heads):
        cols = slice(h * HEAD_DIM, (h + 1) * HEAD_DIM)
        w = jnp.where(causal, ws_ref[h], 0.0).astype(BF16)
        bias = b_ref[:, h:h + 1]
        for c in range(chunks):
            rows = slice(c * HEAD_DIM, (c + 1) * HEAD_DIM)
            mixed = jnp.dot(w, vn[rows, cols], preferred_element_type=F32) + bias
            u = _gelu_tanh(u_ref[rows, cols].astype(F32))
            o_ref[rows, cols] = (u * mixed).astype(o_ref.dtype)


def _sgu_mixer(z_rows, col0, norm_w, w_s, b_t):
    m = z_rows.shape[0]
    heads = w_s.shape[0]
    width = heads * HEAD_DIM
    assert col0 % width == 0 and w_s.shape[1:] == (HEAD_DIM, HEAD_DIM)
    cb = col0 // width
    tm = _tile(m, 256, HEAD_DIM)
    blocks = (3 * _nbytes((tm, width), BF16) + _nbytes((1, width), F32) + _nbytes(w_s.shape, F32)
              + _nbytes((HEAD_DIM, LANES), F32))
    return pl.pallas_call(
        functools.partial(_sgu_kernel, heads=heads, chunks=tm // HEAD_DIM),
        grid=(m // tm,),
        in_specs=[pl.BlockSpec((tm, width), lambda i: (i, cb)),
                  pl.BlockSpec((tm, width), lambda i: (i, cb + 1)),
                  pl.BlockSpec((1, width), lambda i: (0, 0)),
                  pl.BlockSpec(w_s.shape, lambda i: (0, 0, 0)),
                  pl.BlockSpec(b_t.shape, lambda i: (0, 0))],
        out_specs=pl.BlockSpec((tm, width), lambda i: (i, 0)),
        out_shape=jax.ShapeDtypeStruct((m, width), BF16),
        compiler_params=_params(("parallel",), blocks, 4 * _nbytes((tm, width), F32)),
        name="sgu_mixer",
    )(z_rows, z_rows, norm_w, w_s, b_t)


def _neg_abs(x):
    bits = lax.bitcast_convert_type(x, jnp.uint32) | jnp.uint32(0x80000000)
    return lax.bitcast_convert_type(bits, F32)


def _bf16_split(x):
    bits = lax.bitcast_convert_type(x, jnp.uint32) & jnp.uint32(0xFFFF0000)
    hi = lax.bitcast_convert_type(bits, F32)
    return hi, x - hi


def _sb_streams(streams, cumsum_rhs, c2):
    scores = [lax.dot_general(q, kwin, (((1,), (1,)), ((), ())), preferred_element_type=F32)
              for q, kwin, _, _, _ in streams]
    log_betas, sums = [], []
    for s, (_, _, _, _, masks) in zip(scores, streams):
        z2 = s * c2
        sp = jnp.maximum(z2, 0.0) + jnp.log(1.0 + jnp.exp2(_neg_abs(z2))) * LOG2E
        log_betas.append(z2 - sp)
        lhs = []
        for j, mask in enumerate(masks):
            spj = sp[:, j * LANES:(j + 1) * LANES]
            if mask is not None:
                spj = jnp.where(mask, spj, 0.0)
            hi, lo = _bf16_split(spj)
            lhs.append(jnp.concatenate([hi.astype(BF16), lo.astype(BF16)], axis=1))
        lhs = lhs[0] if len(lhs) == 1 else jnp.concatenate(lhs, axis=0)
        sums.append(jnp.dot(lhs, cumsum_rhs, preferred_element_type=F32))
    out = []
    for (q, _, vwin, carry, masks), log_beta, both in zip(streams, log_betas, sums):
        r = q.shape[0]
        weights = [None] * len(masks)
        for j in reversed(range(len(masks))):
            inner = both[j * r:(j + 1) * r, :LANES]
            total = both[j * r:(j + 1) * r, LANES:]
            suffix = inner if carry is None else inner + carry
            pj = jnp.exp2(log_beta[:, j * LANES:(j + 1) * LANES] - suffix)
            if masks[j] is not None:
                pj = jnp.where(masks[j], pj, 0.0)
            weights[j] = pj.astype(BF16)
            carry = total if carry is None else carry + total
        p = weights[0] if len(weights) == 1 else jnp.concatenate(weights, axis=1)
        out.append((jnp.dot(p, vwin, preferred_element_type=F32), carry))
    return out


def _attn_kernel(q_ref, k_ref, v_ref, o_ref, *, heads, n_blocks, peel, group, c2):
    r = Q_ROWS
    krow = lax.broadcasted_iota(jnp.int32, (LANES, LANES), 0)
    kcol = lax.broadcasted_iota(jnp.int32, (LANES, LANES), 1)
    half = jnp.concatenate([(krow > kcol).astype(BF16), jnp.ones((LANES, LANES), BF16)], axis=1)
    cumsum_rhs = jnp.concatenate([half, half], axis=0)
    col = lax.broadcasted_iota(jnp.int32, (r, LANES), 1)
    ahead = col - lax.broadcasted_iota(jnp.int32, (r, LANES), 0)
    run = functools.partial(_sb_streams, cumsum_rhs=cumsum_rhs, c2=c2)

    def span(start, size):
        if isinstance(start, int):
            return slice(start, start + size)
        return pl.ds(pl.multiple_of(start, r), size)

    def run_blocks(blocks):
        def keys_back(b):
            return min(b * r, LEAD_BACK) if isinstance(b, int) else LEAD_BACK

        tasks = [(hh, b, keys_back(b)) for hh in range(heads) for b in blocks]
        streams = []
        for hh, b, back in tasks:
            n = -(-(back + r) // LANES)
            masks = [None if back - j * LANES >= LANES else ahead < back - j * LANES for j in range(n)]
            streams.append((q_ref[hh, span(b * r, r), :], k_ref[hh, span(b * r - back, n * LANES), :],
                            v_ref[hh, span(b * r - back, n * LANES), :], None, masks))
        done = run(streams)
        accs = tuple(acc for acc, _ in done)
        if any(not isinstance(b, int) or b * r - back > 0 for _, b, back in tasks):
            def left_edge(b, back, step):
                return b * r - back - step * LANES

            def cond(st):
                step, carries, _ = st
                least = None
                for (_, b, back), carry in zip(tasks, carries):
                    carry = carry + jnp.where(left_edge(b, back, step) <= 0, OUT_OF_KEYS_CARRY, 0.0)
                    least = carry if least is None else jnp.minimum(least, carry)
                return jnp.min(least) < UNDERFLOW_BITS

            def body(st):
                step, carries, accs = st
                more = []
                for (hh, b, back), carry in zip(tasks, carries):
                    edge = left_edge(b, back, step)
                    start = jnp.maximum(edge - LANES, 0)
                    carry = carry + jnp.where(edge <= 0, OUT_OF_KEYS_CARRY, 0.0)
                    more.append((q_ref[hh, span(b * r, r), :], k_ref[hh, span(start, LANES), :],
                                 v_ref[hh, span(start, LANES), :], carry, [col < edge - start]))
                parts = run(more)
                return (step + 1, tuple(carry for _, carry in parts),
                        tuple(acc + part for acc, (part, _) in zip(accs, parts)))

            _, _, accs = lax.while_loop(cond, body, (0, tuple(carry for _, carry in done), accs))
        for (hh, b, _), acc in zip(tasks, accs):
            o_ref[span(b * r, r), hh * HEAD_DIM:(hh + 1) * HEAD_DIM] = acc.astype(o_ref.dtype)

    for first in range(0, peel, group):
        run_blocks(list(range(first, min(first + group, peel))))

    def group_body(g, _):
        run_blocks([peel + g * group + t for t in range(group)])
        return 0

    if n_blocks > peel:
        lax.fori_loop(0, (n_blocks - peel) // group, group_body, 0)


def _sb_attention(z_heads, n_heads, batch, seq):
    heads = math.gcd(n_heads, 4)
    n_blocks = seq // Q_ROWS
    min_peel = min(-(-LEAD_BACK // Q_ROWS), n_blocks)
    group = min(4, max(n_blocks - min_peel, 1))
    peel = min_peel + (n_blocks - min_peel) % group
    hb = n_heads // heads
    blocks = 4 * _nbytes((heads, seq, HEAD_DIM), BF16)
    kern = functools.partial(_attn_kernel, heads=heads, n_blocks=n_blocks, peel=peel, group=group,
                             c2=HEAD_DIM ** -0.5 * LOG2E)
    return pl.pallas_call(
        kern,
        grid=(batch, hb),
        in_specs=[pl.BlockSpec((None, heads, seq, HEAD_DIM), lambda b, h: (b, h, 0, 0)),
                  pl.BlockSpec((None, heads, seq, HEAD_DIM), lambda b, h: (b, hb + h, 0, 0)),
                  pl.BlockSpec((None, heads, seq, HEAD_DIM), lambda b, h: (b, 2 * hb + h, 0, 0))],
        out_specs=pl.BlockSpec((None, seq, heads * HEAD_DIM), lambda b, h: (b, 0, h)),
        out_shape=jax.ShapeDtypeStruct((batch, seq, n_heads * HEAD_DIM), BF16),
        compiler_params=_params(("parallel", "parallel"), blocks, 8 * MIB),
        name="sb_attention",
    )(z_heads, z_heads, z_heads)


def _out_proj_kernel(yp_ref, ys_ref, yb_ref, w_ref, h_ref, g_ref, ho_ref, hg_ref, rs_ref,
                     ssq_ref, *, pw, sw, nj, d):
    j = pl.program_id(1)

    @pl.when(j == 0)
    def _():
        ssq_ref[...] = jnp.zeros_like(ssq_ref)

    acc = jnp.dot(yp_ref[...], w_ref[0:pw, :], preferred_element_type=F32)
    acc += jnp.dot(ys_ref[...], w_ref[pw:pw + sw, :], preferred_element_type=F32)
    acc += jnp.dot(yb_ref[...], w_ref[pw + sw:, :], preferred_element_type=F32)
    hn = h_ref[...] + acc
    ho_ref[...] = hn
    hg_ref[...] = (hn * g_ref[...]).astype(hg_ref.dtype)
    ssq_ref[...] += jnp.sum(hn * hn, axis=-1, keepdims=True)

    @pl.when(j == nj - 1)
    def _():
        rs_ref[...] = lax.rsqrt(ssq_ref[...] * (1.0 / d) + RMS_EPS)


def _out_proj(h, y_pool, y_sgu, y_sb, w_out, g_next):
    m, d = h.shape
    pw, sw, bw = y_pool.shape[1], y_sgu.shape[1], y_sb.shape[1]
    k = pw + sw + bw
    tm = _tile(m, 1024, 8)
    tn = _tile(d, 512, LANES)
    nj = d // tn
    blocks = (_nbytes((tm, k), BF16) + _nbytes((k, tn), BF16) + 2 * _nbytes((tm, tn), F32)
              + _nbytes((8, tn), F32) + _nbytes((tm, tn), BF16) + _nbytes((tm, LANES), F32))
    return pl.pallas_call(
        functools.partial(_out_proj_kernel, pw=pw, sw=sw, nj=nj, d=d),
        grid=(m // tm, nj),
        in_specs=[pl.BlockSpec((tm, pw), lambda i, j: (i, 0)),
                  pl.BlockSpec((tm, sw), lambda i, j: (i, 0)),
                  pl.BlockSpec((tm, bw), lambda i, j: (i, 0)),
                  pl.BlockSpec((k, tn), lambda i, j: (0, j)),
                  pl.BlockSpec((tm, tn), lambda i, j: (i, j)),
                  pl.BlockSpec((1, tn), lambda i, j: (0, j))],
        out_specs=[pl.BlockSpec((tm, tn), lambda i, j: (i, j)),
                   pl.BlockSpec((tm, tn), lambda i, j: (i, j)),
                   pl.BlockSpec((tm, 1), lambda i, j: (i, 0))],
        out_shape=[jax.ShapeDtypeStruct((m, d), F32), jax.ShapeDtypeStruct((m, d), BF16),
                   jax.ShapeDtypeStruct((m, 1), F32)],
        scratch_shapes=[pltpu.VMEM((tm, 1), F32)],
        compiler_params=_params(("parallel", "arbitrary"), blocks,
                                _nbytes((tm, LANES), F32) + 3 * _nbytes((tm, tn), F32)),
        name="out_proj",
    )(y_pool, y_sgu, y_sb, w_out, h, g_next)


class _CastJobs:
    def __init__(self, jobs, grid):
        self.jobs = jobs
        self.plans = [self.plan(w.shape[1:], grid) for w, _ in jobs]
        assert all(p is not None for p in self.plans)

    @staticmethod
    def plan(shape, grid):
        rows, cols = shape
        for rows_on_first in (True, False):
            gr, gc = grid if rows_on_first else grid[::-1]
            if rows % gr == 0 and cols % gc == 0:
                block = (rows // gr, cols // gc)
                if (block[0] % BF16_SUBLANES == 0 and block[1] % LANES == 0
                        and _nbytes(block, F32) <= CAST_BLOCK_MAX_BYTES):
                    return block, rows_on_first
        return None

    def __len__(self):
        return len(self.jobs)

    def arrays(self):
        return [w for w, _ in self.jobs]

    def in_specs(self):
        return [pl.BlockSpec((None,) + block,
                             (lambda i, j, layer=layer: (layer, i, j)) if first else
                             (lambda i, j, layer=layer: (layer, j, i)))
                for (_, layer), (block, first) in zip(self.jobs, self.plans)]

    def out_specs(self):
        return [pl.BlockSpec(block, (lambda i, j: (i, j)) if first else (lambda i, j: (j, i)))
                for block, first in self.plans]

    def out_shapes(self):
        return [jax.ShapeDtypeStruct(w.shape[1:], BF16) for w, _ in self.jobs]

    def block_bytes(self):
        return sum(_nbytes(block, F32) + _nbytes(block, BF16) for block, _ in self.plans)


def _cast_kernel(w_ref, o_ref):
    o_ref[...] = w_ref[...].astype(o_ref.dtype)


def _cast_layer(w, layer):
    _, rows, cols = w.shape
    tr = _tile(rows, 128, BF16_SUBLANES)
    return pl.pallas_call(
        _cast_kernel,
        grid=(rows // tr,),
        in_specs=[pl.BlockSpec((None, tr, cols), lambda i: (layer, i, 0))],
        out_specs=pl.BlockSpec((tr, cols), lambda i: (i, 0)),
        out_shape=jax.ShapeDtypeStruct((rows, cols), BF16),
        compiler_params=_params(("parallel",), _nbytes((tr, cols), F32) + _nbytes((tr, cols), BF16)),
        name="cast_layer",
    )(w)


def _run_casts(refs, n):
    for src, dst in zip(refs[:n], refs[len(refs) - n:]):
        dst[...] = src[...].astype(dst.dtype)


def _gate_up_kernel(x_ref, rs_ref, wg_ref, wu_ref, *refs, chunks, n_casts):
    o_ref = refs[n_casts]
    rows_per_chunk = x_ref.shape[0] // chunks
    for c in range(chunks):
        rows = slice(c * rows_per_chunk, (c + 1) * rows_per_chunk)
        x = x_ref[rows, :]
        rs = rs_ref[rows, :]
        g = jnp.dot(x, wg_ref[...], preferred_element_type=F32) * rs
        u = jnp.dot(x, wu_ref[...], preferred_element_type=F32)
        o_ref[rows, :] = (g * jax.nn.sigmoid(g) * u).astype(o_ref.dtype)
    _run_casts(refs, n_casts)


def _gate_up_grid(m, f):
    tm = _tile(m, 2048, 8)
    tf = _tile(f, 512, LANES)
    return tm, tf, (m // tm, f // tf)


def _gate_up(hg, rs, w_gate, w_up, cast_jobs):
    m, k = hg.shape
    f = w_gate.shape[1]
    tm, tf, grid = _gate_up_grid(m, f)
    casts = _CastJobs(cast_jobs, grid)
    blocks = (_nbytes((tm, k), BF16) + _nbytes((tm, LANES), F32) + 2 * _nbytes((k, tf), BF16)
              + _nbytes((tm, tf), BF16) + casts.block_bytes())
    return pl.pallas_call(
        functools.partial(_gate_up_kernel, chunks=2 if tm % 1024 == 0 else 1, n_casts=len(casts)),
        grid=grid,
        in_specs=[pl.BlockSpec((tm, k), lambda i, j: (i, 0)),
                  pl.BlockSpec((tm, 1), lambda i, j: (i, 0)),
                  pl.BlockSpec((k, tf), lambda i, j: (0, j)),
                  pl.BlockSpec((k, tf), lambda i, j: (0, j))] + casts.in_specs(),
        out_specs=[pl.BlockSpec((tm, tf), lambda i, j: (i, j))] + casts.out_specs(),
        out_shape=[jax.ShapeDtypeStruct((m, f), BF16)] + casts.out_shapes(),
        compiler_params=_params(("parallel", "arbitrary"), blocks, 3 * _nbytes((tm, tf), F32)),
        name="ffn_gate_up",
    )(hg, rs, w_gate, w_up, *casts.arrays())


def _down_kernel(a_ref, w_ref, h_ref, rs_ref, *refs, n_casts):
    o_ref = refs[n_casts]
    o_ref[...] = h_ref[...] + rs_ref[...] * jnp.dot(a_ref[...], w_ref[...], preferred_element_type=F32)
    _run_casts(refs, n_casts)


def _down_grid(m, d):
    tm = _tile(m, 256, 8)
    tn = _tile(d, 1024, LANES)
    return tm, tn, (d // tn, m // tm)


def _down_proj(h, act, rs, w_down, cast_jobs):
    m, d = h.shape
    f = act.shape[1]
    tm, tn, grid = _down_grid(m, d)
    casts = _CastJobs(cast_jobs, grid)
    blocks = (_nbytes((tm, f), BF16) + 2 * _nbytes((tm, tn), F32) + _nbytes((tm, LANES), F32)
              + casts.block_bytes())
    return pl.pallas_call(
        functools.partial(_down_kernel, n_casts=len(casts)),
        grid=grid,
        in_specs=[pl.BlockSpec((tm, f), lambda j, i: (i, 0)),
                  pl.BlockSpec((f, tn), lambda j, i: (0, j), pipeline_mode=pl.Buffered(1)),
                  pl.BlockSpec((tm, tn), lambda j, i: (i, j)),
                  pl.BlockSpec((tm, 1), lambda j, i: (i, 0))] + casts.in_specs(),
        out_specs=[pl.BlockSpec((tm, tn), lambda j, i: (i, j))] + casts.out_specs(),
        out_shape=[jax.ShapeDtypeStruct((m, d), F32)] + casts.out_shapes(),
        compiler_params=_params(("arbitrary", "arbitrary"), blocks,
                                _nbytes((f, tn), BF16) + _nbytes((tm, tn), F32)),
        name="ffn_down",
    )(act, w_down, h, rs, *casts.arrays())


def _ple_kernel(h_ref, p_ref, gn_ref, gd_ref, gu_ref, pp_ref, gnext_ref, *out_refs):
    h = h_ref[...]
    hg = (h * gn_ref[...]).astype(BF16)
    low = (jnp.dot(hg, gd_ref[...], preferred_element_type=F32) * _rms_scale(h)).astype(BF16)
    gate = jax.nn.sigmoid(jnp.dot(low, gu_ref[...], preferred_element_type=F32))
    emb = jnp.dot(p_ref[...].astype(BF16), pp_ref[...], preferred_element_type=F32)
    hn = h + gate * emb
    xn_ref = out_refs[-1]
    if len(out_refs) == 2:
        out_refs[0][...] = hn
    xn_ref[...] = (hn * _rms_scale(hn) * gnext_ref[...]).astype(xn_ref.dtype)


def _ple(h, p, layer, g_ple, gate_down, gate_up, proj, g_next, last):
    m, d = h.shape
    e = p.shape[2]
    tm = _tile(m, 256, 8)
    row = pl.BlockSpec((tm, d), lambda i: (i, 0))
    vec = pl.BlockSpec((1, d), lambda i: (0, 0))
    if last:
        out_specs = [row]
        out_shape = [jax.ShapeDtypeStruct((m, d), F32)]
        out_bytes = _nbytes((tm, d), F32)
    else:
        out_specs = [row, row]
        out_shape = [jax.ShapeDtypeStruct((m, d), F32), jax.ShapeDtypeStruct((m, d), BF16)]
        out_bytes = _nbytes((tm, d), F32) + _nbytes((tm, d), BF16)
    blocks = (_nbytes((tm, d), F32) + _nbytes((tm, e), F32) + 2 * _nbytes((1, d), F32)
              + 3 * _nbytes((d, e), BF16) + out_bytes)
    return pl.pallas_call(
        _ple_kernel,
        grid=(m // tm,),
        in_specs=[row,
                  pl.BlockSpec((None, tm, e), lambda i: (layer, i, 0)),
                  vec,
                  pl.BlockSpec((d, e), lambda i: (0, 0)),
                  pl.BlockSpec((e, d), lambda i: (0, 0)),
                  pl.BlockSpec((e, d), lambda i: (0, 0)),
                  vec],
        out_specs=out_specs,
        out_shape=out_shape,
        compiler_params=_params(("parallel",), blocks, 4 * _nbytes((tm, d), F32)),
        name="ple_gate",
    )(h, p, g_ple, gate_down, gate_up, proj, g_next)


def kernel(x, p, norm_mix_w, w_in, pool_w, pool_scale, sgu_norm_w, sgu_w, sgu_b, w_out, norm_ffn_w,
           w_gate, w_up, w_down, norm_ple_w, ple_gate_down, ple_gate_up, ple_proj, final_norm_w):
    batch, seq, d = x.shape
    depth = w_in.shape[0]
    m = batch * seq
    pool_width = pool_scale.shape[-1]
    sgu_width = sgu_norm_w.shape[-1]
    sb_width = w_out.shape[1] - pool_width - sgu_width
    n_heads = sb_width // HEAD_DIM
    rows_width = pool_width + 2 * sgu_width
    assert w_in.shape[2] == rows_width + 3 * sb_width and sb_width % HEAD_DIM == 0
    assert seq % LANES == 0 and sgu_w.shape[1] == sgu_width // HEAD_DIM
    assert pool_w.shape[1] == len(POOL_WINDOWS) and pool_w.shape[1] * pool_w.shape[2] == pool_width

    f = w_gate.shape[2]
    gate_up_grid, down_grid = _gate_up_grid(m, f)[2], _down_grid(m, d)[2]
    ride_gate_up = all(_CastJobs.plan(w.shape[1:], gate_up_grid) for w in (w_down, w_gate, w_up))
    ride_down = all(_CastJobs.plan(w.shape[1:], down_grid) for w in (w_in, w_out))

    w_in_b, w_out_b, w_gate_b, w_up_b = (_cast_layer(w, 0) for w in (w_in, w_out, w_gate, w_up))
    h = x.reshape(m, d)
    p_rows = p.reshape(depth, m, p.shape[-1])
    xn = _rmsnorm(h, norm_mix_w[0].reshape(1, d))
    out = None
    for i in range(depth):
        more = i + 1 < depth
        z_rows = _in_proj_rows(xn, w_in_b, 0, rows_width)
        z_heads = _in_proj_heads(xn, w_in_b, rows_width, 3 * sb_width, batch, seq)
        y_pool = _pool_mixer(z_rows, pool_w[i].astype(BF16), pool_scale[i].reshape(1, pool_width), seq)
        y_sgu = _sgu_mixer(z_rows, pool_width, sgu_norm_w[i].reshape(1, sgu_width), sgu_w[i],
                           sgu_b[i].T)
        y_sb = _sb_attention(z_heads, n_heads, batch, seq).reshape(m, sb_width)
        h, hg, rs = _out_proj(h, y_pool, y_sgu, y_sb, w_out_b, norm_ffn_w[i].reshape(1, d))

        jobs = [(w_down, i)] + ([(w_gate, i + 1), (w_up, i + 1)] if more else [])
        act, *cast = _gate_up(hg, rs, w_gate_b, w_up_b, jobs if ride_gate_up else [])
        if ride_gate_up:
            w_down_b = cast[0]
            if more:
                w_gate_b, w_up_b = cast[1], cast[2]
        else:
            w_down_b = _cast_layer(w_down, i)
            if more:
                w_gate_b, w_up_b = _cast_layer(w_gate, i + 1), _cast_layer(w_up, i + 1)

        jobs = [(w_in, i + 1), (w_out, i + 1)] if more and ride_down else []
        h, *cast = _down_proj(h, act, rs, w_down_b, jobs)
        if jobs:
            w_in_b, w_out_b = cast
        elif more:
            w_in_b, w_out_b = _cast_layer(w_in, i + 1), _cast_layer(w_out, i + 1)
        last = i == depth - 1
        g_next = final_norm_w if last else norm_mix_w[i + 1]
        res = _ple(h, p_rows, i, norm_ple_w[i].reshape(1, d), ple_gate_down[i].astype(BF16),
                   ple_gate_up[i].astype(BF16), ple_proj[i].astype(BF16), g_next.reshape(1, d), last)
        if last:
            out = res[0]
        else:
            h, xn = res
    return out.reshape(batch, seq, d)
```

```python
import functools
import math

import jax
import jax.numpy as jnp
from jax import lax
from jax.experimental import pallas as pl
from jax.experimental.pallas import tpu as pltpu

F32 = jnp.float32
BF16 = jnp.bfloat16

RMS_EPS = 1e-6
LN_EPS = 1e-5
POOL_WINDOWS = (2, 4, 8, 16)
POOL_HALO = 16
HEAD_DIM = 128
LANES = 128
BF16_SUBLANES = 16
MIB = 1 << 20
V7X_VMEM_BYTES = 64 * MIB
VMEM_LIMIT_CAP = V7X_VMEM_BYTES - 2 * MIB
INTERNAL_SCRATCH_BYTES = 6 * MIB
CAST_BLOCK_MAX_BYTES = 2 * MIB
LOG2E = math.log2(math.e)
UNDERFLOW_BITS = 150.0
OUT_OF_KEYS_CARRY = 1e30
Q_ROWS = 64
LEAD_BACK = 192


def _tile(n, target, align):
    if n <= target:
        return n
    t = (target // align) * align
    while t >= align:
        if n % t == 0:
            return t
        t -= align
    raise ValueError(f"no tile for n={n} target={target} align={align}")


def _nbytes(shape, dtype):
    return math.prod(shape) * jnp.dtype(dtype).itemsize


def _params(semantics, pipelined_bytes, resident_bytes=0):
    need = 2 * pipelined_bytes + resident_bytes + INTERNAL_SCRATCH_BYTES
    assert need <= V7X_VMEM_BYTES, (need, V7X_VMEM_BYTES)
    return pltpu.CompilerParams(dimension_semantics=semantics, vmem_limit_bytes=VMEM_LIMIT_CAP)


def _rms_scale(x):
    return lax.rsqrt(jnp.mean(x * x, axis=-1, keepdims=True) + RMS_EPS)


def _rmsnorm_kernel(x_ref, g_ref, o_ref):
    x = x_ref[...]
    o_ref[...] = (x * _rms_scale(x) * g_ref[...]).astype(o_ref.dtype)


def _rmsnorm(x, g):
    m, d = x.shape
    tm = _tile(m, 512, 8)
    return pl.pallas_call(
        _rmsnorm_kernel,
        grid=(m // tm,),
        in_specs=[pl.BlockSpec((tm, d), lambda i: (i, 0)), pl.BlockSpec((1, d), lambda i: (0, 0))],
        out_specs=pl.BlockSpec((tm, d), lambda i: (i, 0)),
        out_shape=jax.ShapeDtypeStruct((m, d), BF16),
        compiler_params=_params(("parallel",), _nbytes((tm, d), F32) + _nbytes((tm, d), BF16),
                                _nbytes((tm, d), F32)),
        name="rmsnorm",
    )(x, g)


def _matmul_kernel(a_ref, w_ref, o_ref):
    o_ref[...] = jnp.dot(a_ref[...], w_ref[...], preferred_element_type=F32).astype(o_ref.dtype)


def _in_proj_rows(xn, w, col0, ncols):
    m, k = xn.shape
    tm = _tile(m, 1024, 8)
    tn = _tile(math.gcd(ncols, col0) if col0 else ncols, 1536, LANES)
    off = col0 // tn
    blocks = _nbytes((tm, k), BF16) + _nbytes((k, tn), BF16) + _nbytes((tm, tn), BF16)
    return pl.pallas_call(
        _matmul_kernel,
        grid=(m // tm, ncols // tn),
        in_specs=[pl.BlockSpec((tm, k), lambda i, j: (i, 0)),
                  pl.BlockSpec((k, tn), lambda i, j: (0, j + off))],
        out_specs=pl.BlockSpec((tm, tn), lambda i, j: (i, j)),
        out_shape=jax.ShapeDtypeStruct((m, ncols), BF16),
        compiler_params=_params(("parallel", "arbitrary"), blocks, _nbytes((tm, tn), F32)),
        name="in_proj_rows",
    )(xn, w)


def _matmul_heads_kernel(a_ref, w_ref, o_ref, *, heads_per_tile):
    res = jnp.dot(a_ref[...], w_ref[...], preferred_element_type=F32)
    for hh in range(heads_per_tile):
        o_ref[hh] = res[:, hh * HEAD_DIM:(hh + 1) * HEAD_DIM].astype(o_ref.dtype)


def _in_proj_heads(xn, w, col0, ncols, batch, seq):
    m, k = xn.shape
    tm = _tile(seq, 1024, 8)
    tn = _tile(math.gcd(ncols, col0), 1536, LANES)
    off = col0 // tn
    hpt = tn // HEAD_DIM
    spt = seq // tm
    blocks = _nbytes((tm, k), BF16) + _nbytes((k, tn), BF16) + _nbytes((tm, tn), BF16)
    return pl.pallas_call(
        functools.partial(_matmul_heads_kernel, heads_per_tile=hpt),
        grid=(m // tm, ncols // tn),
        in_specs=[pl.BlockSpec((tm, k), lambda i, j: (i, 0)),
                  pl.BlockSpec((k, tn), lambda i, j: (0, j + off))],
        out_specs=pl.BlockSpec((None, hpt, tm, HEAD_DIM), lambda i, j: (i // spt, j, i % spt, 0)),
        out_shape=jax.ShapeDtypeStruct((batch, ncols // HEAD_DIM, seq, HEAD_DIM), BF16),
        compiler_params=_params(("parallel", "arbitrary"), blocks, _nbytes((tm, tn), F32)),
        name="in_proj_heads",
    )(xn, w)


def _pool_kernel(a_ref, halo_ref, pw_ref, ps_ref, o_ref, ext_ref, *, seq, tm, group):
    t0 = lax.rem(pl.program_id(0) * tm, seq)
    a = a_ref[...].astype(F32)
    ext_ref[0:POOL_HALO, :] = jnp.where(t0 == 0, 0.0, halo_ref[...].astype(F32))
    ext_ref[POOL_HALO:, :] = a
    pos = t0 + lax.broadcasted_iota(jnp.int32, (tm, 1), 0)
    for g, w in enumerate(POOL_WINDOWS):
        cols = slice(g * group, (g + 1) * group)
        win = a[:, cols]
        for back in range(1, w):
            win = win + ext_ref[POOL_HALO - back:POOL_HALO - back + tm, cols]
        cnt = jnp.minimum(pos + 1, w).astype(F32)
        pooled = win / cnt - a[:, cols]
        y = jnp.dot(pooled.astype(BF16), pw_ref[g], preferred_element_type=F32)
        o_ref[:, cols] = (y * ps_ref[:, cols]).astype(o_ref.dtype)


def _pool_mixer(z_rows, pool_w, pool_scale, seq):
    m = z_rows.shape[0]
    n_groups, group, _ = pool_w.shape
    width = n_groups * group
    tm = _tile(seq, 512, POOL_HALO)
    hpt = tm // POOL_HALO
    blocks = (_nbytes((tm, width), BF16) * 2 + _nbytes((POOL_HALO, width), BF16)
              + _nbytes(pool_w.shape, BF16) + _nbytes((1, width), F32))
    return pl.pallas_call(
        functools.partial(_pool_kernel, seq=seq, tm=tm, group=group),
        grid=(m // tm,),
        in_specs=[pl.BlockSpec((tm, width), lambda i: (i, 0)),
                  pl.BlockSpec((POOL_HALO, width), lambda i: (jnp.maximum(i * hpt - 1, 0), 0)),
                  pl.BlockSpec(pool_w.shape, lambda i: (0, 0, 0)),
                  pl.BlockSpec((1, width), lambda i: (0, 0))],
        out_specs=pl.BlockSpec((tm, width), lambda i: (i, 0)),
        out_shape=jax.ShapeDtypeStruct((m, width), BF16),
        scratch_shapes=[pltpu.VMEM((tm + POOL_HALO, width), F32)],
        compiler_params=_params(("parallel",), blocks, 4 * _nbytes((tm + POOL_HALO, width), F32)),
        name="pool_mixer",
    )(z_rows, z_rows, pool_w, pool_scale)


def _gelu_tanh(x):
    c = math.sqrt(2.0 / math.pi)
    half = 0.5 * x
    return half + half * jnp.tanh(x * (c + (c * 0.044715) * (x * x)))


def _sgu_kernel(u_ref, v_ref, nw_ref, ws_ref, b_ref, o_ref, *, heads, chunks):
    v = _gelu_tanh(v_ref[...].astype(F32))
    vc = v - jnp.mean(v, axis=-1, keepdims=True)
    var = jnp.mean(vc * vc, axis=-1, keepdims=True)
    vn = (vc * lax.rsqrt(var + LN_EPS) * nw_ref[...]).astype(BF16)
    row = lax.broadcasted_iota(jnp.int32, (HEAD_DIM, HEAD_DIM), 0)
    col = lax.broadcasted_iota(jnp.int32, (HEAD_DIM, HEAD_DIM), 1)
    causal = row >= col
    for h in range(heads):
        cols = slice(h * HEAD_DIM, (h + 1) * HEAD_DIM)
        w = jnp.where(causal, ws_ref[h], 0.0).astype(BF16)
        bias = b_ref[:, h:h + 1]
        for c in range(chunks):
            rows = slice(c * HEAD_DIM, (c + 1) * HEAD_DIM)
            mixed = jnp.dot(w, vn[rows, cols], preferred_element_type=F32) + bias
            u = _gelu_tanh(u_ref[rows, cols].astype(F32))
            o_ref[rows, cols] = (u * mixed).astype(o_ref.dtype)


def _sgu_mixer(z_rows, col0, norm_w, w_s, b_t):
    m = z_rows.shape[0]
    heads = w_s.shape[0]
    width = heads * HEAD_DIM
    assert col0 % width == 0 and w_s.shape[1:] == (HEAD_DIM, HEAD_DIM)
    cb = col0 // width
    tm = _tile(m, 256, HEAD_DIM)
    blocks = (3 * _nbytes((tm, width), BF16) + _nbytes((1, width), F32) + _nbytes(w_s.shape, F32)
              + _nbytes((HEAD_DIM, LANES), F32))
    return pl.pallas_call(
        functools.partial(_sgu_kernel, heads=heads, chunks=tm // HEAD_DIM),
        grid=(m // tm,),
        in_specs=[pl.BlockSpec((tm, width), lambda i: (i, cb)),
                  pl.BlockSpec((tm, width), lambda i: (i, cb + 1)),
                  pl.BlockSpec((1, width), lambda i: (0, 0)),
                  pl.BlockSpec(w_s.shape, lambda i: (0, 0, 0)),
                  pl.BlockSpec(b_t.shape, lambda i: (0, 0))],
        out_specs=pl.BlockSpec((tm, width), lambda i: (i, 0)),
        out_shape=jax.ShapeDtypeStruct((m, width), BF16),
        compiler_params=_params(("parallel",), blocks, 4 * _nbytes((tm, width), F32)),
        name="sgu_mixer",
    )(z_rows, z_rows, norm_w, w_s, b_t)


def _neg_abs(x):
    bits = lax.bitcast_convert_type(x, jnp.uint32) | jnp.uint32(0x80000000)
    return lax.bitcast_convert_type(bits, F32)


def _bf16_split(x):
    bits = lax.bitcast_convert_type(x, jnp.uint32) & jnp.uint32(0xFFFF0000)
    hi = lax.bitcast_convert_type(bits, F32)
    return hi, x - hi


def _sb_streams(streams, cumsum_rhs, c2):
    scores = [lax.dot_general(q, kwin, (((1,), (1,)), ((), ())), preferred_element_type=F32)
              for q, kwin, _, _, _ in streams]
    log_betas, sums = [], []
    for s, (_, _, _, _, masks) in zip(scores, streams):
        z2 = s * c2
        sp = jnp.maximum(z2, 0.0) + jnp.log(1.0 + jnp.exp2(_neg_abs(z2))) * LOG2E
        log_betas.append(z2 - sp)
        lhs = []
        for j, mask in enumerate(masks):
            spj = sp[:, j * LANES:(j + 1) * LANES]
            if mask is not None:
                spj = jnp.where(mask, spj, 0.0)
            hi, lo = _bf16_split(spj)
            lhs.append(jnp.concatenate([hi.astype(BF16), lo.astype(BF16)], axis=1))
        lhs = lhs[0] if len(lhs) == 1 else jnp.concatenate(lhs, axis=0)
        sums.append(jnp.dot(lhs, cumsum_rhs, preferred_element_type=F32))
    out = []
    for (q, _, vwin, carry, masks), log_beta, both in zip(streams, log_betas, sums):
        r = q.shape[0]
        weights = [None] * len(masks)
        for j in reversed(range(len(masks))):
            inner = both[j * r:(j + 1) * r, :LANES]
            total = both[j * r:(j + 1) * r, LANES:]
            suffix = inner if carry is None else inner + carry
            pj = jnp.exp2(log_beta[:, j * LANES:(j + 1) * LANES] - suffix)
            if masks[j] is not None:
                pj = jnp.where(masks[j], pj, 0.0)
            weights[j] = pj.astype(BF16)
            carry = total if carry is None else carry + total
        p = weights[0] if len(weights) == 1 else jnp.concatenate(weights, axis=1)
        out.append((jnp.dot(p, vwin, preferred_element_type=F32), carry))
    return out


def _attn_kernel(q_ref, k_ref, v_ref, o_ref, *, heads, n_blocks, peel, group, c2):
    r = Q_ROWS
    krow = lax.broadcasted_iota(jnp.int32, (LANES, LANES), 0)
    kcol = lax.broadcasted_iota(jnp.int32, (LANES, LANES), 1)
    half = jnp.concatenate([(krow > kcol).astype(BF16), jnp.ones((LANES, LANES), BF16)], axis=1)
    cumsum_rhs = jnp.concatenate([half, half], axis=0)
    col = lax.broadcasted_iota(jnp.int32, (r, LANES), 1)
    ahead = col - lax.broadcasted_iota(jnp.int32, (r, LANES), 0)
    run = functools.partial(_sb_streams, cumsum_rhs=cumsum_rhs, c2=c2)

    def span(start, size):
        if isinstance(start, int):
            return slice(start, start + size)
        return pl.ds(pl.multiple_of(start, r), size)

    def run_blocks(blocks):
        def keys_back(b):
            return min(b * r, LEAD_BACK) if isinstance(b, int) else LEAD_BACK

        tasks = [(hh, b, keys_back(b)) for hh in range(heads) for b in blocks]
        streams = []
        for hh, b, back in tasks:
            n = -(-(back + r) // LANES)
            masks = [None if back - j * LANES >= LANES else ahead < back - j * LANES for j in range(n)]
            streams.append((q_ref[hh, span(b * r, r), :], k_ref[hh, span(b * r - back, n * LANES), :],
                            v_ref[hh, span(b * r - back, n * LANES), :], None, masks))
        done = run(streams)
        accs = tuple(acc for acc, _ in done)
        if any(not isinstance(b, int) or b * r - back > 0 for _, b, back in tasks):
            def left_edge(b, back, step):
                return b * r - back - step * LANES

            def cond(st):
                step, carries, _ = st
                least = None
                for (_, b, back), carry in zip(tasks, carries):
                    carry = carry + jnp.where(left_edge(b, back, step) <= 0, OUT_OF_KEYS_CARRY, 0.0)
                    least = carry if least is None else jnp.minimum(least, carry)
                return jnp.min(least) < UNDERFLOW_BITS

            def body(st):
                step, carries, accs = st
                more = []
                for (hh, b, back), carry in zip(tasks, carries):
                    edge = left_edge(b, back, step)
                    start = jnp.maximum(edge - LANES, 0)
                    carry = carry + jnp.where(edge <= 0, OUT_OF_KEYS_CARRY, 0.0)
                    more.append((q_ref[hh, span(b * r, r), :], k_ref[hh, span(start, LANES), :],
                                 v_ref[hh, span(start, LANES), :], carry, [col < edge - start]))
                parts = run(more)
                return (step + 1, tuple(carry for _, carry in parts),
                        tuple(acc + part for acc, (part, _) in zip(accs, parts)))

            _, _, accs = lax.while_loop(cond, body, (0, tuple(carry for _, carry in done), accs))
        for (hh, b, _), acc in zip(tasks, accs):
            o_ref[span(b * r, r), hh * HEAD_DIM:(hh + 1) * HEAD_DIM] = acc.astype(o_ref.dtype)

    for first in range(0, peel, group):
        run_blocks(list(range(first, min(first + group, peel))))

    def group_body(g, _):
        run_blocks([peel + g * group + t for t in range(group)])
        return 0

    if n_blocks > peel:
        lax.fori_loop(0, (n_blocks - peel) // group, group_body, 0)


def _sb_attention(z_heads, n_heads, batch, seq):
    heads = math.gcd(n_heads, 4)
    n_blocks = seq // Q_ROWS
    min_peel = min(-(-LEAD_BACK // Q_ROWS), n_blocks)
    group = min(4, max(n_blocks - min_peel, 1))
    peel = min_peel + (n_blocks - min_peel) % group
    hb = n_heads // heads
    blocks = 4 * _nbytes((heads, seq, HEAD_DIM), BF16)
    kern = functools.partial(_attn_kernel, heads=heads, n_blocks=n_blocks, peel=peel, group=group,
                             c2=HEAD_DIM ** -0.5 * LOG2E)
    return pl.pallas_call(
        kern,
        grid=(batch, hb),
        in_specs=[pl.BlockSpec((None, heads, seq, HEAD_DIM), lambda b, h: (b, h, 0, 0)),
                  pl.BlockSpec((None, heads, seq, HEAD_DIM), lambda b, h: (b, hb + h, 0, 0)),
                  pl.BlockSpec((None, heads, seq, HEAD_DIM), lambda b, h: (b, 2 * hb + h, 0, 0))],
        out_specs=pl.BlockSpec((None, seq, heads * HEAD_DIM), lambda b, h: (b, 0, h)),
        out_shape=jax.ShapeDtypeStruct((batch, seq, n_heads * HEAD_DIM), BF16),
        compiler_params=_params(("parallel", "parallel"), blocks, 8 * MIB),
        name="sb_attention",
    )(z_heads, z_heads, z_heads)


def _out_proj_kernel(yp_ref, ys_ref, yb_ref, w_ref, h_ref, g_ref, ho_ref, hg_ref, rs_ref,
                     ssq_ref, *, pw, sw, nj, d):
    j = pl.program_id(1)

    @pl.when(j == 0)
    def _():
        ssq_ref[...] = jnp.zeros_like(ssq_ref)

    acc = jnp.dot(yp_ref[...], w_ref[0:pw, :], preferred_element_type=F32)
    acc += jnp.dot(ys_ref[...], w_ref[pw:pw + sw, :], preferred_element_type=F32)
    acc += jnp.dot(yb_ref[...], w_ref[pw + sw:, :], preferred_element_type=F32)
    hn = h_ref[...] + acc
    ho_ref[...] = hn
    hg_ref[...] = (hn * g_ref[...]).astype(hg_ref.dtype)
    ssq_ref[...] += jnp.sum(hn * hn, axis=-1, keepdims=True)

    @pl.when(j == nj - 1)
    def _():
        rs_ref[...] = lax.rsqrt(ssq_ref[...] * (1.0 / d) + RMS_EPS)


def _out_proj(h, y_pool, y_sgu, y_sb, w_out, g_next):
    m, d = h.shape
    pw, sw, bw = y_pool.shape[1], y_sgu.shape[1], y_sb.shape[1]
    k = pw + sw + bw
    tm = _tile(m, 1024, 8)
    tn = _tile(d, 1024, LANES)
    nj = d // tn
    blocks = (_nbytes((tm, k), BF16) + _nbytes((k, tn), BF16) + 2 * _nbytes((tm, tn), F32)
              + _nbytes((8, tn), F32) + _nbytes((tm, tn), BF16) + _nbytes((tm, LANES), F32))
    return pl.pallas_call(
        functools.partial(_out_proj_kernel, pw=pw, sw=sw, nj=nj, d=d),
        grid=(m // tm, nj),
        in_specs=[pl.BlockSpec((tm, pw), lambda i, j: (i, 0)),
                  pl.BlockSpec((tm, sw), lambda i, j: (i, 0)),
                  pl.BlockSpec((tm, bw), lambda i, j: (i, 0)),
                  pl.BlockSpec((k, tn), lambda i, j: (0, j)),
                  pl.BlockSpec((tm, tn), lambda i, j: (i, j)),
                  pl.BlockSpec((1, tn), lambda i, j: (0, j))],
        out_specs=[pl.BlockSpec((tm, tn), lambda i, j: (i, j)),
                   pl.BlockSpec((tm, tn), lambda i, j: (i, j)),
                   pl.BlockSpec((tm, 1), lambda i, j: (i, 0))],
        out_shape=[jax.ShapeDtypeStruct((m, d), F32), jax.ShapeDtypeStruct((m, d), BF16),
                   jax.ShapeDtypeStruct((m, 1), F32)],
        scratch_shapes=[pltpu.VMEM((tm, 1), F32)],
        compiler_params=_params(("parallel", "arbitrary"), blocks,
                                _nbytes((tm, LANES), F32) + _nbytes((tm, tn), F32)),
        name="out_proj",
    )(y_pool, y_sgu, y_sb, w_out, h, g_next)


class _CastJobs:
    def __init__(self, jobs, grid):
        self.jobs = jobs
        self.plans = [self.plan(w.shape[1:], grid) for w, _ in jobs]
        assert all(p is not None for p in self.plans)

    @staticmethod
    def plan(shape, grid):
        rows, cols = shape
        for rows_on_first in (True, False):
            gr, gc = grid if rows_on_first else grid[::-1]
            if rows % gr == 0 and cols % gc == 0:
                block = (rows // gr, cols // gc)
                if (block[0] % BF16_SUBLANES == 0 and block[1] % LANES == 0
                        and _nbytes(block, F32) <= CAST_BLOCK_MAX_BYTES):
                    return block, rows_on_first
        return None

    def __len__(self):
        return len(self.jobs)

    def arrays(self):
        return [w for w, _ in self.jobs]

    def in_specs(self):
        return [pl.BlockSpec((None,) + block,
                             (lambda i, j, layer=layer: (layer, i, j)) if first else
                             (lambda i, j, layer=layer: (layer, j, i)))
                for (_, layer), (block, first) in zip(self.jobs, self.plans)]

    def out_specs(self):
        return [pl.BlockSpec(block, (lambda i, j: (i, j)) if first else (lambda i, j: (j, i)))
                for block, first in self.plans]

    def out_shapes(self):
        return [jax.ShapeDtypeStruct(w.shape[1:], BF16) for w, _ in self.jobs]

    def block_bytes(self):
        return sum(_nbytes(block, F32) + _nbytes(block, BF16) for block, _ in self.plans)


def _cast_kernel(w_ref, o_ref):
    o_ref[...] = w_ref[...].astype(o_ref.dtype)


def _cast_layer(w, layer):
    _, rows, cols = w.shape
    tr = _tile(rows, 128, BF16_SUBLANES)
    return pl.pallas_call(
        _cast_kernel,
        grid=(rows // tr,),
        in_specs=[pl.BlockSpec((None, tr, cols), lambda i: (layer, i, 0))],
        out_specs=pl.BlockSpec((tr, cols), lambda i: (i, 0)),
        out_shape=jax.ShapeDtypeStruct((rows, cols), BF16),
        compiler_params=_params(("parallel",), _nbytes((tr, cols), F32) + _nbytes((tr, cols), BF16)),
        name="cast_layer",
    )(w)


def _run_casts(refs, n):
    for src, dst in zip(refs[:n], refs[len(refs) - n:]):
        dst[...] = src[...].astype(dst.dtype)


def _gate_up_kernel(x_ref, rs_ref, wg_ref, wu_ref, *refs, chunks, n_casts):
    o_ref = refs[n_casts]
    rows_per_chunk = x_ref.shape[0] // chunks
    for c in range(chunks):
        rows = slice(c * rows_per_chunk, (c + 1) * rows_per_chunk)
        x = x_ref[rows, :]
        rs = rs_ref[rows, :]
        g = jnp.dot(x, wg_ref[...], preferred_element_type=F32) * rs
        u = jnp.dot(x, wu_ref[...], preferred_element_type=F32)
        o_ref[rows, :] = (g * jax.nn.sigmoid(g) * u).astype(o_ref.dtype)
    _run_casts(refs, n_casts)


def _gate_up_grid(m, f):
    tm = _tile(m, 2048, 8)
    tf = _tile(f, 512, LANES)
    return tm, tf, (m // tm, f // tf)


def _gate_up(hg, rs, w_gate, w_up, cast_jobs):
    m, k = hg.shape
    f = w_gate.shape[1]
    tm, tf, grid = _gate_up_grid(m, f)
    casts = _CastJobs(cast_jobs, grid)
    blocks = (_nbytes((tm, k), BF16) + _nbytes((tm, LANES), F32) + 2 * _nbytes((k, tf), BF16)
              + _nbytes((tm, tf), BF16) + casts.block_bytes())
    return pl.pallas_call(
        functools.partial(_gate_up_kernel, chunks=2 if tm % 1024 == 0 else 1, n_casts=len(casts)),
        grid=grid,
        in_specs=[pl.BlockSpec((tm, k), lambda i, j: (i, 0)),
                  pl.BlockSpec((tm, 1), lambda i, j: (i, 0)),
                  pl.BlockSpec((k, tf), lambda i, j: (0, j)),
                  pl.BlockSpec((k, tf), lambda i, j: (0, j))] + casts.in_specs(),
        out_specs=[pl.BlockSpec((tm, tf), lambda i, j: (i, j))] + casts.out_specs(),
        out_shape=[jax.ShapeDtypeStruct((m, f), BF16)] + casts.out_shapes(),
        compiler_params=_params(("parallel", "arbitrary"), blocks, 3 * _nbytes((tm, tf), F32)),
        name="ffn_gate_up",
    )(hg, rs, w_gate, w_up, *casts.arrays())


def _down_kernel(a_ref, w_ref, h_ref, rs_ref, *refs, n_casts):
    o_ref = refs[n_casts]
    o_ref[...] = h_ref[...] + rs_ref[...] * jnp.dot(a_ref[...], w_ref[...], preferred_element_type=F32)
    _run_casts(refs, n_casts)


def _down_grid(m, d):
    tm = _tile(m, 256, 8)
    tn = _tile(d, 1024, LANES)
    return tm, tn, (d // tn, m // tm)


def _down_proj(h, act, rs, w_down, cast_jobs):
    m, d = h.shape
    f = act.shape[1]
    tm, tn, grid = _down_grid(m, d)
    casts = _CastJobs(cast_jobs, grid)
    blocks = (_nbytes((tm, f), BF16) + 2 * _nbytes((tm, tn), F32) + _nbytes((tm, LANES), F32)
              + casts.block_bytes())
    return pl.pallas_call(
        functools.partial(_down_kernel, n_casts=len(casts)),
        grid=grid,
        in_specs=[pl.BlockSpec((tm, f), lambda j, i: (i, 0)),
                  pl.BlockSpec((f, tn), lambda j, i: (0, j), pipeline_mode=pl.Buffered(1)),
                  pl.BlockSpec((tm, tn), lambda j, i: (i, j)),
                  pl.BlockSpec((tm, 1), lambda j, i: (i, 0))] + casts.in_specs(),
        out_specs=[pl.BlockSpec((tm, tn), lambda j, i: (i, j))] + casts.out_specs(),
        out_shape=[jax.ShapeDtypeStruct((m, d), F32)] + casts.out_shapes(),
        compiler_params=_params(("arbitrary", "arbitrary"), blocks,
                                _nbytes((f, tn), BF16) + _nbytes((tm, tn), F32)),
        name="ffn_down",
    )(act, w_down, h, rs, *casts.arrays())


def _ple_kernel(h_ref, p_ref, gn_ref, gd_ref, gu_ref, pp_ref, gnext_ref, *out_refs):
    h = h_ref[...]
    hg = (h * gn_ref[...]).astype(BF16)
    low = (jnp.dot(hg, gd_ref[...], preferred_element_type=F32) * _rms_scale(h)).astype(BF16)
    gate = jax.nn.sigmoid(jnp.dot(low, gu_ref[...], preferred_element_type=F32))
    emb = jnp.dot(p_ref[...].astype(BF16), pp_ref[...], preferred_element_type=F32)
    hn = h + gate * emb
    xn_ref = out_refs[-1]
    if len(out_refs) == 2:
        out_refs[0][...] = hn
    xn_ref[...] = (hn * _rms_scale(hn) * gnext_ref[...]).astype(xn_ref.dtype)


def _ple(h, p, layer, g_ple, gate_down, gate_up, proj, g_next, last):
    m, d = h.shape
    e = p.shape[2]
    tm = _tile(m, 256, 8)
    row = pl.BlockSpec((tm, d), lambda i: (i, 0))
    vec = pl.BlockSpec((1, d), lambda i: (0, 0))
    if last:
        out_specs = [row]
        out_shape = [jax.ShapeDtypeStruct((m, d), F32)]
        out_bytes = _nbytes((tm, d), F32)
    else:
        out_specs = [row, row]
        out_shape = [jax.ShapeDtypeStruct((m, d), F32), jax.ShapeDtypeStruct((m, d), BF16)]
        out_bytes = _nbytes((tm, d), F32) + _nbytes((tm, d), BF16)
    blocks = (_nbytes((tm, d), F32) + _nbytes((tm, e), F32) + 2 * _nbytes((1, d), F32)
              + 3 * _nbytes((d, e), BF16) + out_bytes)
    return pl.pallas_call(
        _ple_kernel,
        grid=(m // tm,),
        in_specs=[row,
                  pl.BlockSpec((None, tm, e), lambda i: (layer, i, 0)),
                  vec,
                  pl.BlockSpec((d, e), lambda i: (0, 0)),
                  pl.BlockSpec((e, d), lambda i: (0, 0)),
                  pl.BlockSpec((e, d), lambda i: (0, 0)),
                  vec],
        out_specs=out_specs,
        out_shape=out_shape,
        compiler_params=_params(("parallel",), blocks, 4 * _nbytes((tm, d), F32)),
        name="ple_gate",
    )(h, p, g_ple, gate_down, gate_up, proj, g_next)


def kernel(x, p, norm_mix_w, w_in, pool_w, pool_scale, sgu_norm_w, sgu_w, sgu_b, w_out, norm_ffn_w,
           w_gate, w_up, w_down, norm_ple_w, ple_gate_down, ple_gate_up, ple_proj, final_norm_w):
    batch, seq, d = x.shape
    depth = w_in.shape[0]
    m = batch * seq
    pool_width = pool_scale.shape[-1]
    sgu_width = sgu_norm_w.shape[-1]
    sb_width = w_out.shape[1] - pool_width - sgu_width
    n_heads = sb_width // HEAD_DIM
    rows_width = pool_width + 2 * sgu_width
    assert w_in.shape[2] == rows_width + 3 * sb_width and sb_width % HEAD_DIM == 0
    assert seq % LANES == 0 and sgu_w.shape[1] == sgu_width // HEAD_DIM
    assert pool_w.shape[1] == len(POOL_WINDOWS) and pool_w.shape[1] * pool_w.shape[2] == pool_width

    f = w_gate.shape[2]
    gate_up_grid, down_grid = _gate_up_grid(m, f)[2], _down_grid(m, d)[2]
    ride_gate_up = all(_CastJobs.plan(w.shape[1:], gate_up_grid) for w in (w_down, w_gate, w_up))
    ride_down = all(_CastJobs.plan(w.shape[1:], down_grid) for w in (w_in, w_out))

    w_in_b, w_out_b, w_gate_b, w_up_b = (_cast_layer(w, 0) for w in (w_in, w_out, w_gate, w_up))
    h = x.reshape(m, d)
    p_rows = p.reshape(depth, m, p.shape[-1])
    xn = _rmsnorm(h, norm_mix_w[0].reshape(1, d))
    out = None
    for i in range(depth):
        more = i + 1 < depth
        z_rows = _in_proj_rows(xn, w_in_b, 0, rows_width)
        z_heads = _in_proj_heads(xn, w_in_b, rows_width, 3 * sb_width, batch, seq)
        y_pool = _pool_mixer(z_rows, pool_w[i].astype(BF16), pool_scale[i].reshape(1, pool_width), seq)
        y_sgu = _sgu_mixer(z_rows, pool_width, sgu_norm_w[i].reshape(1, sgu_width), sgu_w[i],
                           sgu_b[i].T)
        y_sb = _sb_attention(z_heads, n_heads, batch, seq).reshape(m, sb_width)
        h, hg, rs = _out_proj(h, y_pool, y_sgu, y_sb, w_out_b, norm_ffn_w[i].reshape(1, d))

        jobs = [(w_down, i)] + ([(w_gate, i + 1), (w_up, i + 1)] if more else [])
        act, *cast = _gate_up(hg, rs, w_gate_b, w_up_b, jobs if ride_gate_up else [])
        if ride_gate_up:
            w_down_b = cast[0]
            if more:
                w_gate_b, w_up_b = cast[1], cast[2]
        else:
            w_down_b = _cast_layer(w_down, i)
            if more:
                w_gate_b, w_up_b = _cast_layer(w_gate, i + 1), _cast_layer(w_up, i + 1)

        jobs = [(w_in, i + 1), (w_out, i + 1)] if more and ride_down else []
        h, *cast = _down_proj(h, act, rs, w_down_b, jobs)
        if jobs:
            w_in_b, w_out_b = cast
        elif more:
            w_in_b, w_out_b = _cast_layer(w_in, i + 1), _cast_layer(w_out, i + 1)
        last = i == depth - 1
        g_next = final_norm_w if last else norm_mix_w[i + 1]
        res = _ple(h, p_rows, i, norm_ple_w[i].reshape(1, d), ple_gate_down[i].astype(BF16),
                   ple_gate_up[i].astype(BF16), ple_proj[i].astype(BF16), g_next.reshape(1, d), last)
        if last:
            out = res[0]
        else:
            h, xn = res
    return out.reshape(batch, seq, d)
```

```python
import functools
import math

import jax
import jax.numpy as jnp
from jax import lax
from jax.experimental import pallas as pl
from jax.experimental.pallas import tpu as pltpu

F32 = jnp.float32
BF16 = jnp.bfloat16

RMS_EPS = 1e-6
LN_EPS = 1e-5
POOL_WINDOWS = (2, 4, 8, 16)
POOL_HALO = 16
HEAD_DIM = 128
LANES = 128
BF16_SUBLANES = 16
MIB = 1 << 20
V7X_VMEM_BYTES = 64 * MIB
VMEM_LIMIT_CAP = V7X_VMEM_BYTES - 2 * MIB
INTERNAL_SCRATCH_BYTES = 2 * MIB
CAST_BLOCK_MAX_BYTES = 2 * MIB
LOG2E = math.log2(math.e)
UNDERFLOW_BITS = 150.0
OUT_OF_KEYS_CARRY = 1e30
Q_ROWS = 64
LEAD_BACK = 192


def _tile(n, target, align):
    if n <= target:
        return n
    t = (target // align) * align
    while t >= align:
        if n % t == 0:
            return t
        t -= align
    raise ValueError(f"no tile for n={n} target={target} align={align}")


def _nbytes(shape, dtype):
    return math.prod(shape) * jnp.dtype(dtype).itemsize


def _params(semantics, pipelined_bytes, resident_bytes=0):
    need = 2 * pipelined_bytes + resident_bytes + INTERNAL_SCRATCH_BYTES
    assert need <= V7X_VMEM_BYTES, (need, V7X_VMEM_BYTES)
    return pltpu.CompilerParams(dimension_semantics=semantics, vmem_limit_bytes=VMEM_LIMIT_CAP)


def _rms_scale(x):
    return lax.rsqrt(jnp.mean(x * x, axis=-1, keepdims=True) + RMS_EPS)


def _rmsnorm_kernel(x_ref, g_ref, o_ref):
    x = x_ref[...]
    o_ref[...] = (x * _rms_scale(x) * g_ref[...]).astype(o_ref.dtype)


def _rmsnorm(x, g):
    m, d = x.shape
    tm = _tile(m, 512, 8)
    return pl.pallas_call(
        _rmsnorm_kernel,
        grid=(m // tm,),
        in_specs=[pl.BlockSpec((tm, d), lambda i: (i, 0)), pl.BlockSpec((1, d), lambda i: (0, 0))],
        out_specs=pl.BlockSpec((tm, d), lambda i: (i, 0)),
        out_shape=jax.ShapeDtypeStruct((m, d), BF16),
        compiler_params=_params(("parallel",), _nbytes((tm, d), F32) + _nbytes((tm, d), BF16),
                                _nbytes((tm, d), F32)),
        name="rmsnorm",
    )(x, g)


def _matmul_kernel(a_ref, w_ref, o_ref):
    o_ref[...] = jnp.dot(a_ref[...], w_ref[...], preferred_element_type=F32).astype(o_ref.dtype)


def _in_proj_rows(xn, w, col0, ncols):
    m, k = xn.shape
    tm = _tile(m, 1024, 8)
    tn = _tile(math.gcd(ncols, col0) if col0 else ncols, 1536, LANES)
    off = col0 // tn
    blocks = _nbytes((tm, k), BF16) + _nbytes((k, tn), BF16) + _nbytes((tm, tn), BF16)
    return pl.pallas_call(
        _matmul_kernel,
        grid=(m // tm, ncols // tn),
        in_specs=[pl.BlockSpec((tm, k), lambda i, j: (i, 0)),
                  pl.BlockSpec((k, tn), lambda i, j: (0, j + off))],
        out_specs=pl.BlockSpec((tm, tn), lambda i, j: (i, j)),
        out_shape=jax.ShapeDtypeStruct((m, ncols), BF16),
        compiler_params=_params(("parallel", "arbitrary"), blocks, _nbytes((tm, tn), F32)),
        name="in_proj_rows",
    )(xn, w)


def _matmul_heads_kernel(a_ref, w_ref, o_ref, *, heads_per_tile):
    res = jnp.dot(a_ref[...], w_ref[...], preferred_element_type=F32)
    for hh in range(heads_per_tile):
        o_ref[hh] = res[:, hh * HEAD_DIM:(hh + 1) * HEAD_DIM].astype(o_ref.dtype)


def _in_proj_heads(xn, w, col0, ncols, batch, seq):
    m, k = xn.shape
    tm = _tile(seq, 1024, 8)
    tn = _tile(math.gcd(ncols, col0), 1536, LANES)
    off = col0 // tn
    hpt = tn // HEAD_DIM
    spt = seq // tm
    blocks = _nbytes((tm, k), BF16) + _nbytes((k, tn), BF16) + _nbytes((tm, tn), BF16)
    return pl.pallas_call(
        functools.partial(_matmul_heads_kernel, heads_per_tile=hpt),
        grid=(m // tm, ncols // tn),
        in_specs=[pl.BlockSpec((tm, k), lambda i, j: (i, 0)),
                  pl.BlockSpec((k, tn), lambda i, j: (0, j + off))],
        out_specs=pl.BlockSpec((None, hpt, tm, HEAD_DIM), lambda i, j: (i // spt, j, i % spt, 0)),
        out_shape=jax.ShapeDtypeStruct((batch, ncols // HEAD_DIM, seq, HEAD_DIM), BF16),
        compiler_params=_params(("parallel", "arbitrary"), blocks, _nbytes((tm, tn), F32)),
        name="in_proj_heads",
    )(xn, w)


def _pool_kernel(a_ref, halo_ref, pw_ref, ps_ref, o_ref, ext_ref, *, seq, tm, group):
    t0 = lax.rem(pl.program_id(0) * tm, seq)
    a = a_ref[...].astype(F32)
    ext_ref[0:POOL_HALO, :] = jnp.where(t0 == 0, 0.0, halo_ref[...].astype(F32))
    ext_ref[POOL_HALO:, :] = a
    pos = t0 + lax.broadcasted_iota(jnp.int32, (tm, 1), 0)
    for g, w in enumerate(POOL_WINDOWS):
        cols = slice(g * group, (g + 1) * group)
        win = a[:, cols]
        for back in range(1, w):
            win = win + ext_ref[POOL_HALO - back:POOL_HALO - back + tm, cols]
        cnt = jnp.minimum(pos + 1, w).astype(F32)
        pooled = win / cnt - a[:, cols]
        y = jnp.dot(pooled.astype(BF16), pw_ref[g], preferred_element_type=F32)
        o_ref[:, cols] = (y * ps_ref[:, cols]).astype(o_ref.dtype)


def _pool_mixer(z_rows, pool_w, pool_scale, seq):
    m = z_rows.shape[0]
    n_groups, group, _ = pool_w.shape
    width = n_groups * group
    tm = _tile(seq, 512, POOL_HALO)
    hpt = tm // POOL_HALO
    blocks = (_nbytes((tm, width), BF16) * 2 + _nbytes((POOL_HALO, width), BF16)
              + _nbytes(pool_w.shape, BF16) + _nbytes((1, width), F32))
    return pl.pallas_call(
        functools.partial(_pool_kernel, seq=seq, tm=tm, group=group),
        grid=(m // tm,),
        in_specs=[pl.BlockSpec((tm, width), lambda i: (i, 0)),
                  pl.BlockSpec((POOL_HALO, width), lambda i: (jnp.maximum(i * hpt - 1, 0), 0)),
                  pl.BlockSpec(pool_w.shape, lambda i: (0, 0, 0)),
                  pl.BlockSpec((1, width), lambda i: (0, 0))],
        out_specs=pl.BlockSpec((tm, width), lambda i: (i, 0)),
        out_shape=jax.ShapeDtypeStruct((m, width), BF16),
        scratch_shapes=[pltpu.VMEM((tm + POOL_HALO, width), F32)],
        compiler_params=_params(("parallel",), blocks, 4 * _nbytes((tm + POOL_HALO, width), F32)),
        name="pool_mixer",
    )(z_rows, z_rows, pool_w, pool_scale)


def _gelu_tanh(x):
    c = math.sqrt(2.0 / math.pi)
    half = 0.5 * x
    return half + half * jnp.tanh(x * (c + (c * 0.044715) * (x * x)))


def _sgu_kernel(u_ref, v_ref, nw_ref, ws_ref, b_ref, o_ref, *, heads, chunks):
    v = _gelu_tanh(v_ref[...].astype(F32))
    vc = v - jnp.mean(v, axis=-1, keepdims=True)
    var = jnp.mean(vc * vc, axis=-1, keepdims=True)
    vn = (vc * lax.rsqrt(var + LN_EPS) * nw_ref[...]).astype(BF16)
    row = lax.broadcasted_iota(jnp.int32, (HEAD_DIM, HEAD_DIM), 0)
    col = lax.broadcasted_iota(jnp.int32, (HEAD_DIM, HEAD_DIM), 1)
    causal = row >= col
    for h in range(heads):
        cols = slice(h * HEAD_DIM, (h + 1) * HEAD_DIM)
        w = jnp.where(causal, ws_ref[h], 0.0).astype(BF16)
        bias = b_ref[:, h:h + 1]
        for c in range(chunks):
            rows = slice(c * HEAD_DIM, (c + 1) * HEAD_DIM)
            mixed = jnp.dot(w, vn[rows, cols], preferred_element_type=F32) + bias
            u = _gelu_tanh(u_ref[rows, cols].astype(F32))
            o_ref[rows, cols] = (u * mixed).astype(o_ref.dtype)


def _sgu_mixer(z_rows, col0, norm_w, w_s, b_t):
    m = z_rows.shape[0]
    heads = w_s.shape[0]
    width = heads * HEAD_DIM
    assert col0 % width == 0 and w_s.shape[1:] == (HEAD_DIM, HEAD_DIM)
    cb = col0 // width
    tm = _tile(m, 256, HEAD_DIM)
    blocks = (3 * _nbytes((tm, width), BF16) + _nbytes((1, width), F32) + _nbytes(w_s.shape, F32)
              + _nbytes((HEAD_DIM, LANES), F32))
    return pl.pallas_call(
        functools.partial(_sgu_kernel, heads=heads, chunks=tm // HEAD_DIM),
        grid=(m // tm,),
        in_specs=[pl.BlockSpec((tm, width), lambda i: (i, cb)),
                  pl.BlockSpec((tm, width), lambda i: (i, cb + 1)),
                  pl.BlockSpec((1, width), lambda i: (0, 0)),
                  pl.BlockSpec(w_s.shape, lambda i: (0, 0, 0)),
                  pl.BlockSpec(b_t.shape, lambda i: (0, 0))],
        out_specs=pl.BlockSpec((tm, width), lambda i: (i, 0)),
        out_shape=jax.ShapeDtypeStruct((m, width), BF16),
        compiler_params=_params(("parallel",), blocks, 4 * _nbytes((tm, width), F32)),
        name="sgu_mixer",
    )(z_rows, z_rows, norm_w, w_s, b_t)


def _neg_abs(x):
    bits = lax.bitcast_convert_type(x, jnp.uint32) | jnp.uint32(0x80000000)
    return lax.bitcast_convert_type(bits, F32)


def _bf16_split(x):
    bits = lax.bitcast_convert_type(x, jnp.uint32) & jnp.uint32(0xFFFF0000)
    hi = lax.bitcast_convert_type(bits, F32)
    return hi, x - hi


def _sb_streams(streams, cumsum_rhs, c2):
    scores = [lax.dot_general(q, kwin, (((1,), (1,)), ((), ())), preferred_element_type=F32)
              for q, kwin, _, _, _ in streams]
    log_betas, sums = [], []
    for s, (_, _, _, _, masks) in zip(scores, streams):
        z2 = s * c2
        sp = jnp.maximum(z2, 0.0) + jnp.log(1.0 + jnp.exp2(_neg_abs(z2))) * LOG2E
        log_betas.append(z2 - sp)
        lhs = []
        for j, mask in enumerate(masks):
            spj = sp[:, j * LANES:(j + 1) * LANES]
            if mask is not None:
                spj = jnp.where(mask, spj, 0.0)
            hi, lo = _bf16_split(spj)
            lhs.append(jnp.concatenate([hi.astype(BF16), lo.astype(BF16)], axis=1))
        lhs = lhs[0] if len(lhs) == 1 else jnp.concatenate(lhs, axis=0)
        sums.append(jnp.dot(lhs, cumsum_rhs, preferred_element_type=F32))
    out = []
    for (q, _, vwin, carry, masks), log_beta, both in zip(streams, log_betas, sums):
        r = q.shape[0]
        weights = [None] * len(masks)
        for j in reversed(range(len(masks))):
            inner = both[j * r:(j + 1) * r, :LANES]
            total = both[j * r:(j + 1) * r, LANES:]
            suffix = inner if carry is None else inner + carry
            pj = jnp.exp2(log_beta[:, j * LANES:(j + 1) * LANES] - suffix)
            if masks[j] is not None:
                pj = jnp.where(masks[j], pj, 0.0)
            weights[j] = pj.astype(BF16)
            carry = total if carry is None else carry + total
        p = weights[0] if len(weights) == 1 else jnp.concatenate(weights, axis=1)
        out.append((jnp.dot(p, vwin, preferred_element_type=F32), carry))
    return out


def _attn_kernel(q_ref, k_ref, v_ref, o_ref, *, heads, n_blocks, peel, group, c2):
    r = Q_ROWS
    krow = lax.broadcasted_iota(jnp.int32, (LANES, LANES), 0)
    kcol = lax.broadcasted_iota(jnp.int32, (LANES, LANES), 1)
    half = jnp.concatenate([(krow > kcol).astype(BF16), jnp.ones((LANES, LANES), BF16)], axis=1)
    cumsum_rhs = jnp.concatenate([half, half], axis=0)
    col = lax.broadcasted_iota(jnp.int32, (r, LANES), 1)
    ahead = col - lax.broadcasted_iota(jnp.int32, (r, LANES), 0)
    run = functools.partial(_sb_streams, cumsum_rhs=cumsum_rhs, c2=c2)

    def span(start, size):
        if isinstance(start, int):
            return slice(start, start + size)
        return pl.ds(pl.multiple_of(start, r), size)

    def run_blocks(blocks):
        def keys_back(b):
            return min(b * r, LEAD_BACK) if isinstance(b, int) else LEAD_BACK

        tasks = [(hh, b, keys_back(b)) for hh in range(heads) for b in blocks]
        streams = []
        for hh, b, back in tasks:
            n = -(-(back + r) // LANES)
            masks = [None if back - j * LANES >= LANES else ahead < back - j * LANES for j in range(n)]
            streams.append((q_ref[hh, span(b * r, r), :], k_ref[hh, span(b * r - back, n * LANES), :],
                            v_ref[hh, span(b * r - back, n * LANES), :], None, masks))
        done = run(streams)
        accs = tuple(acc for acc, _ in done)
        if any(not isinstance(b, int) or b * r - back > 0 for _, b, back in tasks):
            def left_edge(b, back, step):
                return b * r - back - step * LANES

            def cond(st):
                step, carries, _ = st
                least = None
                for (_, b, back), carry in zip(tasks, carries):
                    carry = carry + jnp.where(left_edge(b, back, step) <= 0, OUT_OF_KEYS_CARRY, 0.0)
                    least = carry if least is None else jnp.minimum(least, carry)
                return jnp.min(least) < UNDERFLOW_BITS

            def body(st):
                step, carries, accs = st
                more = []
                for (hh, b, back), carry in zip(tasks, carries):
                    edge = left_edge(b, back, step)
                    start = jnp.maximum(edge - LANES, 0)
                    carry = carry + jnp.where(edge <= 0, OUT_OF_KEYS_CARRY, 0.0)
                    more.append((q_ref[hh, span(b * r, r), :], k_ref[hh, span(start, LANES), :],
                                 v_ref[hh, span(start, LANES), :], carry, [col < edge - start]))
                parts = run(more)
                return (step + 1, tuple(carry for _, carry in parts),
                        tuple(acc + part for acc, (part, _) in zip(accs, parts)))

            _, _, accs = lax.while_loop(cond, body, (0, tuple(carry for _, carry in done), accs))
        for (hh, b, _), acc in zip(tasks, accs):
            o_ref[span(b * r, r), hh * HEAD_DIM:(hh + 1) * HEAD_DIM] = acc.astype(o_ref.dtype)

    for first in range(0, peel, group):
        run_blocks(list(range(first, min(first + group, peel))))

    def group_body(g, _):
        run_blocks([peel + g * group + t for t in range(group)])
        return 0

    if n_blocks > peel:
        lax.fori_loop(0, (n_blocks - peel) // group, group_body, 0)


def _sb_attention(z_heads, n_heads, batch, seq):
    heads = math.gcd(n_heads, 4)
    n_blocks = seq // Q_ROWS
    min_peel = min(-(-LEAD_BACK // Q_ROWS), n_blocks)
    group = min(4, max(n_blocks - min_peel, 1))
    peel = min_peel + (n_blocks - min_peel) % group
    hb = n_heads // heads
    blocks = 4 * _nbytes((heads, seq, HEAD_DIM), BF16)
    kern = functools.partial(_attn_kernel, heads=heads, n_blocks=n_blocks, peel=peel, group=group,
                             c2=HEAD_DIM ** -0.5 * LOG2E)
    return pl.pallas_call(
        kern,
        grid=(batch, hb),
        in_specs=[pl.BlockSpec((None, heads, seq, HEAD_DIM), lambda b, h: (b, h, 0, 0)),
                  pl.BlockSpec((None, heads, seq, HEAD_DIM), lambda b, h: (b, hb + h, 0, 0)),
                  pl.BlockSpec((None, heads, seq, HEAD_DIM), lambda b, h: (b, 2 * hb + h, 0, 0))],
        out_specs=pl.BlockSpec((None, seq, heads * HEAD_DIM), lambda b, h: (b, 0, h)),
        out_shape=jax.ShapeDtypeStruct((batch, seq, n_heads * HEAD_DIM), BF16),
        compiler_params=_params(("parallel", "parallel"), blocks, 8 * MIB),
        name="sb_attention",
    )(z_heads, z_heads, z_heads)


def _out_proj_kernel(yp_ref, ys_ref, yb_ref, w_ref, h_ref, g_ref, ho_ref, hg_ref, rs_ref,
                     ssq_ref, *, pw, sw, nj, d):
    j = pl.program_id(1)

    @pl.when(j == 0)
    def _():
        ssq_ref[...] = jnp.zeros_like(ssq_ref)

    acc = jnp.dot(yp_ref[...], w_ref[0:pw, :], preferred_element_type=F32)
    acc += jnp.dot(ys_ref[...], w_ref[pw:pw + sw, :], preferred_element_type=F32)
    acc += jnp.dot(yb_ref[...], w_ref[pw + sw:, :], preferred_element_type=F32)
    hn = h_ref[...] + acc
    ho_ref[...] = hn
    hg_ref[...] = (hn * g_ref[...]).astype(hg_ref.dtype)
    ssq_ref[...] += jnp.sum(hn * hn, axis=-1, keepdims=True)

    @pl.when(j == nj - 1)
    def _():
        rs_ref[...] = lax.rsqrt(ssq_ref[...] * (1.0 / d) + RMS_EPS)


def _out_proj(h, y_pool, y_sgu, y_sb, w_out, g_next):
    m, d = h.shape
    pw, sw, bw = y_pool.shape[1], y_sgu.shape[1], y_sb.shape[1]
    k = pw + sw + bw
    tm = _tile(m, 1024, 8)
    tn = _tile(d, 1024, LANES)
    nj = d // tn
    blocks = (_nbytes((tm, k), BF16) + _nbytes((k, tn), BF16) + 2 * _nbytes((tm, tn), F32)
              + _nbytes((8, tn), F32) + _nbytes((tm, tn), BF16) + _nbytes((tm, LANES), F32))
    return pl.pallas_call(
        functools.partial(_out_proj_kernel, pw=pw, sw=sw, nj=nj, d=d),
        grid=(m // tm, nj),
        in_specs=[pl.BlockSpec((tm, pw), lambda i, j: (i, 0)),
                  pl.BlockSpec((tm, sw), lambda i, j: (i, 0)),
                  pl.BlockSpec((tm, bw), lambda i, j: (i, 0)),
                  pl.BlockSpec((k, tn), lambda i, j: (0, j)),
                  pl.BlockSpec((tm, tn), lambda i, j: (i, j)),
                  pl.BlockSpec((1, tn), lambda i, j: (0, j))],
        out_specs=[pl.BlockSpec((tm, tn), lambda i, j: (i, j)),
                   pl.BlockSpec((tm, tn), lambda i, j: (i, j)),
                   pl.BlockSpec((tm, 1), lambda i, j: (i, 0))],
        out_shape=[jax.ShapeDtypeStruct((m, d), F32), jax.ShapeDtypeStruct((m, d), BF16),
                   jax.ShapeDtypeStruct((m, 1), F32)],
        scratch_shapes=[pltpu.VMEM((tm, 1), F32)],
        compiler_params=_params(("parallel", "arbitrary"), blocks,
                                _nbytes((tm, LANES), F32) + _nbytes((tm, tn), F32)),
        name="out_proj",
    )(y_pool, y_sgu, y_sb, w_out, h, g_next)


class _CastJobs:
    def __init__(self, jobs, grid):
        self.jobs = jobs
        self.plans = [self.plan(w.shape[1:], grid) for w, _ in jobs]
        assert all(p is not None for p in self.plans)

    @staticmethod
    def plan(shape, grid):
        rows, cols = shape
        for rows_on_first in (True, False):
            gr, gc = grid if rows_on_first else grid[::-1]
            if rows % gr == 0 and cols % gc == 0:
                block = (rows // gr, cols // gc)
                if (block[0] % BF16_SUBLANES == 0 and block[1] % LANES == 0
                        and _nbytes(block, F32) <= CAST_BLOCK_MAX_BYTES):
                    return block, rows_on_first
        return None

    def __len__(self):
        return len(self.jobs)

    def arrays(self):
        return [w for w, _ in self.jobs]

    def in_specs(self):
        return [pl.BlockSpec((None,) + block,
                             (lambda i, j, layer=layer: (layer, i, j)) if first else
                             (lambda i, j, layer=layer: (layer, j, i)))
                for (_, layer), (block, first) in zip(self.jobs, self.plans)]

    def out_specs(self):
        return [pl.BlockSpec(block, (lambda i, j: (i, j)) if first else (lambda i, j: (j, i)))
                for block, first in self.plans]

    def out_shapes(self):
        return [jax.ShapeDtypeStruct(w.shape[1:], BF16) for w, _ in self.jobs]

    def block_bytes(self):
        return sum(_nbytes(block, F32) + _nbytes(block, BF16) for block, _ in self.plans)


def _cast_kernel(w_ref, o_ref):
    o_ref[...] = w_ref[...].astype(o_ref.dtype)


def _cast_layer(w, layer):
    _, rows, cols = w.shape
    tr = _tile(rows, 128, BF16_SUBLANES)
    return pl.pallas_call(
        _cast_kernel,
        grid=(rows // tr,),
        in_specs=[pl.BlockSpec((None, tr, cols), lambda i: (layer, i, 0))],
        out_specs=pl.BlockSpec((tr, cols), lambda i: (i, 0)),
        out_shape=jax.ShapeDtypeStruct((rows, cols), BF16),
        compiler_params=_params(("parallel",), _nbytes((tr, cols), F32) + _nbytes((tr, cols), BF16)),
        name="cast_layer",
    )(w)


def _run_casts(refs, n):
    for src, dst in zip(refs[:n], refs[len(refs) - n:]):
        dst[...] = src[...].astype(dst.dtype)


def _gate_up_kernel(x_ref, rs_ref, wg_ref, wu_ref, *refs, chunks, n_casts):
    o_ref = refs[n_casts]
    rows_per_chunk = x_ref.shape[0] // chunks
    for c in range(chunks):
        rows = slice(c * rows_per_chunk, (c + 1) * rows_per_chunk)
        x = x_ref[rows, :]
        rs = rs_ref[rows, :]
        g = jnp.dot(x, wg_ref[...], preferred_element_type=F32) * rs
        u = jnp.dot(x, wu_ref[...], preferred_element_type=F32)
        o_ref[rows, :] = (g * jax.nn.sigmoid(g) * u).astype(o_ref.dtype)
    _run_casts(refs, n_casts)


def _gate_up_grid(m, f):
    tm = _tile(m, 2048, 8)
    tf = _tile(f, 512, LANES)
    return tm, tf, (m // tm, f // tf)


def _gate_up(hg, rs, w_gate, w_up, cast_jobs):
    m, k = hg.shape
    f = w_gate.shape[1]
    tm, tf, grid = _gate_up_grid(m, f)
    casts = _CastJobs(cast_jobs, grid)
    blocks = (_nbytes((tm, k), BF16) + _nbytes((tm, LANES), F32) + 2 * _nbytes((k, tf), BF16)
              + _nbytes((tm, tf), BF16) + casts.block_bytes())
    return pl.pallas_call(
        functools.partial(_gate_up_kernel, chunks=2 if tm % 1024 == 0 else 1, n_casts=len(casts)),
        grid=grid,
        in_specs=[pl.BlockSpec((tm, k), lambda i, j: (i, 0)),
                  pl.BlockSpec((tm, 1), lambda i, j: (i, 0)),
                  pl.BlockSpec((k, tf), lambda i, j: (0, j)),
                  pl.BlockSpec((k, tf), lambda i, j: (0, j))] + casts.in_specs(),
        out_specs=[pl.BlockSpec((tm, tf), lambda i, j: (i, j))] + casts.out_specs(),
        out_shape=[jax.ShapeDtypeStruct((m, f), BF16)] + casts.out_shapes(),
        compiler_params=_params(("parallel", "arbitrary"), blocks, 3 * _nbytes((tm, tf), F32)),
        name="ffn_gate_up",
    )(hg, rs, w_gate, w_up, *casts.arrays())


def _down_kernel(a_ref, w_ref, h_ref, rs_ref, *refs, n_casts):
    o_ref = refs[n_casts]
    o_ref[...] = h_ref[...] + rs_ref[...] * jnp.dot(a_ref[...], w_ref[...], preferred_element_type=F32)
    _run_casts(refs, n_casts)


def _down_grid(m, d):
    tm = _tile(m, 512, 8)
    tn = _tile(d, 1024, LANES)
    return tm, tn, (d // tn, m // tm)


def _down_proj(h, act, rs, w_down, cast_jobs):
    m, d = h.shape
    f = act.shape[1]
    tm, tn, grid = _down_grid(m, d)
    casts = _CastJobs(cast_jobs, grid)
    blocks = (_nbytes((tm, f), BF16) + 2 * _nbytes((tm, tn), F32) + _nbytes((tm, LANES), F32)
              + casts.block_bytes())
    return pl.pallas_call(
        functools.partial(_down_kernel, n_casts=len(casts)),
        grid=grid,
        in_specs=[pl.BlockSpec((tm, f), lambda j, i: (i, 0)),
                  pl.BlockSpec((f, tn), lambda j, i: (0, j), pipeline_mode=pl.Buffered(1)),
                  pl.BlockSpec((tm, tn), lambda j, i: (i, j)),
                  pl.BlockSpec((tm, 1), lambda j, i: (i, 0))] + casts.in_specs(),
        out_specs=[pl.BlockSpec((tm, tn), lambda j, i: (i, j))] + casts.out_specs(),
        out_shape=[jax.ShapeDtypeStruct((m, d), F32)] + casts.out_shapes(),
        compiler_params=_params(("arbitrary", "arbitrary"), blocks,
                                _nbytes((f, tn), BF16) + _nbytes((tm, tn), F32)),
        name="ffn_down",
    )(act, w_down, h, rs, *casts.arrays())


def _ple_kernel(h_ref, p_ref, gn_ref, gd_ref, gu_ref, pp_ref, gnext_ref, *out_refs):
    h = h_ref[...]
    hg = (h * gn_ref[...]).astype(BF16)
    low = (jnp.dot(hg, gd_ref[...], preferred_element_type=F32) * _rms_scale(h)).astype(BF16)
    gate = jax.nn.sigmoid(jnp.dot(low, gu_ref[...], preferred_element_type=F32))
    emb = jnp.dot(p_ref[...].astype(BF16), pp_ref[...], preferred_element_type=F32)
    hn = h + gate * emb
    xn_ref = out_refs[-1]
    if len(out_refs) == 2:
        out_refs[0][...] = hn
    xn_ref[...] = (hn * _rms_scale(hn) * gnext_ref[...]).astype(xn_ref.dtype)


def _ple(h, p, layer, g_ple, gate_down, gate_up, proj, g_next, last):
    m, d = h.shape
    e = p.shape[2]
    tm = _tile(m, 256, 8)
    row = pl.BlockSpec((tm, d), lambda i: (i, 0))
    vec = pl.BlockSpec((1, d), lambda i: (0, 0))
    if last:
        out_specs = [row]
        out_shape = [jax.ShapeDtypeStruct((m, d), F32)]
        out_bytes = _nbytes((tm, d), F32)
    else:
        out_specs = [row, row]
        out_shape = [jax.ShapeDtypeStruct((m, d), F32), jax.ShapeDtypeStruct((m, d), BF16)]
        out_bytes = _nbytes((tm, d), F32) + _nbytes((tm, d), BF16)
    blocks = (_nbytes((tm, d), F32) + _nbytes((tm, e), F32) + 2 * _nbytes((1, d), F32)
              + 3 * _nbytes((d, e), BF16) + out_bytes)
    return pl.pallas_call(
        _ple_kernel,
        grid=(m // tm,),
        in_specs=[row,
                  pl.BlockSpec((None, tm, e), lambda i: (layer, i, 0)),
                  vec,
                  pl.BlockSpec((d, e), lambda i: (0, 0)),
                  pl.BlockSpec((e, d), lambda i: (0, 0)),
                  pl.BlockSpec((e, d), lambda i: (0, 0)),
                  vec],
        out_specs=out_specs,
        out_shape=out_shape,
        compiler_params=_params(("parallel",), blocks, 4 * _nbytes((tm, d), F32)),
        name="ple_gate",
    )(h, p, g_ple, gate_down, gate_up, proj, g_next)


def kernel(x, p, norm_mix_w, w_in, pool_w, pool_scale, sgu_norm_w, sgu_w, sgu_b, w_out, norm_ffn_w,
           w_gate, w_up, w_down, norm_ple_w, ple_gate_down, ple_gate_up, ple_proj, final_norm_w):
    batch, seq, d = x.shape
    depth = w_in.shape[0]
    m = batch * seq
    pool_width = pool_scale.shape[-1]
    sgu_width = sgu_norm_w.shape[-1]
    sb_width = w_out.shape[1] - pool_width - sgu_width
    n_heads = sb_width // HEAD_DIM
    rows_width = pool_width + 2 * sgu_width
    assert w_in.shape[2] == rows_width + 3 * sb_width and sb_width % HEAD_DIM == 0
    assert seq % LANES == 0 and sgu_w.shape[1] == sgu_width // HEAD_DIM
    assert pool_w.shape[1] == len(POOL_WINDOWS) and pool_w.shape[1] * pool_w.shape[2] == pool_width

    f = w_gate.shape[2]
    gate_up_grid, down_grid = _gate_up_grid(m, f)[2], _down_grid(m, d)[2]
    ride_gate_up = all(_CastJobs.plan(w.shape[1:], gate_up_grid) for w in (w_down, w_gate, w_up))
    ride_down = all(_CastJobs.plan(w.shape[1:], down_grid) for w in (w_in, w_out))

    w_in_b, w_out_b, w_gate_b, w_up_b = (_cast_layer(w, 0) for w in (w_in, w_out, w_gate, w_up))
    h = x.reshape(m, d)
    p_rows = p.reshape(depth, m, p.shape[-1])
    xn = _rmsnorm(h, norm_mix_w[0].reshape(1, d))
    out = None
    for i in range(depth):
        more = i + 1 < depth
        z_rows = _in_proj_rows(xn, w_in_b, 0, rows_width)
        z_heads = _in_proj_heads(xn, w_in_b, rows_width, 3 * sb_width, batch, seq)
        y_pool = _pool_mixer(z_rows, pool_w[i].astype(BF16), pool_scale[i].reshape(1, pool_width), seq)
        y_sgu = _sgu_mixer(z_rows, pool_width, sgu_norm_w[i].reshape(1, sgu_width), sgu_w[i],
                           sgu_b[i].T)
        y_sb = _sb_attention(z_heads, n_heads, batch, seq).reshape(m, sb_width)
        h, hg, rs = _out_proj(h, y_pool, y_sgu, y_sb, w_out_b, norm_ffn_w[i].reshape(1, d))

        jobs = [(w_down, i)] + ([(w_gate, i + 1), (w_up, i + 1)] if more else [])
        act, *cast = _gate_up(hg, rs, w_gate_b, w_up_b, jobs if ride_gate_up else [])
        if ride_gate_up:
            w_down_b = cast[0]
            if more:
                w_gate_b, w_up_b = cast[1], cast[2]
        else:
            w_down_b = _cast_layer(w_down, i)
            if more:
                w_gate_b, w_up_b = _cast_layer(w_gate, i + 1), _cast_layer(w_up, i + 1)

        jobs = [(w_in, i + 1), (w_out, i + 1)] if more and ride_down else []
        h, *cast = _down_proj(h, act, rs, w_down_b, jobs)
        if jobs:
            w_in_b, w_out_b = cast
        elif more:
            w_in_b, w_out_b = _cast_layer(w_in, i + 1), _cast_layer(w_out, i + 1)
        last = i == depth - 1
        g_next = final_norm_w if last else norm_mix_w[i + 1]
        res = _ple(h, p_rows, i, norm_ple_w[i].reshape(1, d), ple_gate_down[i].astype(BF16),
                   ple_gate_up[i].astype(BF16), ple_proj[i].astype(BF16), g_next.reshape(1, d), last)
        if last:
            out = res[0]
        else:
            h, xn = res
    return out.reshape(batch, seq, d)
```

```python
import functools
import math

import jax
import jax.numpy as jnp
from jax import lax
from jax.experimental import pallas as pl
from jax.experimental.pallas import tpu as pltpu

F32 = jnp.float32
BF16 = jnp.bfloat16

RMS_EPS = 1e-6
LN_EPS = 1e-5
POOL_WINDOWS = (2, 4, 8, 16)
POOL_HALO = 16
HEAD_DIM = 128
LANES = 128
BF16_SUBLANES = 16
MIB = 1 << 20
V7X_VMEM_BYTES = 64 * MIB
VMEM_LIMIT_CAP = V7X_VMEM_BYTES - 2 * MIB
INTERNAL_SCRATCH_BYTES = 2 * MIB
CAST_BLOCK_MAX_BYTES = 2 * MIB
LOG2E = math.log2(math.e)
UNDERFLOW_BITS = 150.0
OUT_OF_KEYS_CARRY = 1e30
Q_ROWS = 64
LEAD_BACK = 192


def _tile(n, target, align):
    if n <= target:
        return n
    t = (target // align) * align
    while t >= align:
        if n % t == 0:
            return t
        t -= align
    raise ValueError(f"no tile for n={n} target={target} align={align}")


def _nbytes(shape, dtype):
    return math.prod(shape) * jnp.dtype(dtype).itemsize


def _params(semantics, pipelined_bytes, resident_bytes=0):
    need = 2 * pipelined_bytes + resident_bytes + INTERNAL_SCRATCH_BYTES
    assert need <= V7X_VMEM_BYTES, (need, V7X_VMEM_BYTES)
    return pltpu.CompilerParams(dimension_semantics=semantics, vmem_limit_bytes=VMEM_LIMIT_CAP)


def _rms_scale(x):
    return lax.rsqrt(jnp.mean(x * x, axis=-1, keepdims=True) + RMS_EPS)


def _rmsnorm_kernel(x_ref, g_ref, o_ref):
    x = x_ref[...]
    o_ref[...] = (x * _rms_scale(x) * g_ref[...]).astype(o_ref.dtype)


def _rmsnorm(x, g):
    m, d = x.shape
    tm = _tile(m, 512, 8)
    return pl.pallas_call(
        _rmsnorm_kernel,
        grid=(m // tm,),
        in_specs=[pl.BlockSpec((tm, d), lambda i: (i, 0)), pl.BlockSpec((1, d), lambda i: (0, 0))],
        out_specs=pl.BlockSpec((tm, d), lambda i: (i, 0)),
        out_shape=jax.ShapeDtypeStruct((m, d), BF16),
        compiler_params=_params(("parallel",), _nbytes((tm, d), F32) + _nbytes((tm, d), BF16),
                                _nbytes((tm, d), F32)),
        name="rmsnorm",
    )(x, g)


def _matmul_kernel(a_ref, w_ref, o_ref):
    o_ref[...] = jnp.dot(a_ref[...], w_ref[...], preferred_element_type=F32).astype(o_ref.dtype)


def _in_proj_rows(xn, w, col0, ncols):
    m, k = xn.shape
    tm = _tile(m, 1024, 8)
    tn = _tile(math.gcd(ncols, col0) if col0 else ncols, 1536, LANES)
    off = col0 // tn
    blocks = _nbytes((tm, k), BF16) + _nbytes((k, tn), BF16) + _nbytes((tm, tn), BF16)
    return pl.pallas_call(
        _matmul_kernel,
        grid=(m // tm, ncols // tn),
        in_specs=[pl.BlockSpec((tm, k), lambda i, j: (i, 0)),
                  pl.BlockSpec((k, tn), lambda i, j: (0, j + off))],
        out_specs=pl.BlockSpec((tm, tn), lambda i, j: (i, j)),
        out_shape=jax.ShapeDtypeStruct((m, ncols), BF16),
        compiler_params=_params(("parallel", "arbitrary"), blocks, _nbytes((tm, tn), F32)),
        name="in_proj_rows",
    )(xn, w)


def _matmul_heads_kernel(a_ref, w_ref, o_ref, *, heads_per_tile):
    res = jnp.dot(a_ref[...], w_ref[...], preferred_element_type=F32)
    for hh in range(heads_per_tile):
        o_ref[hh] = res[:, hh * HEAD_DIM:(hh + 1) * HEAD_DIM].astype(o_ref.dtype)


def _in_proj_heads(xn, w, col0, ncols, batch, seq):
    m, k = xn.shape
    tm = _tile(seq, 1024, 8)
    tn = _tile(math.gcd(ncols, col0), 1536, LANES)
    off = col0 // tn
    hpt = tn // HEAD_DIM
    spt = seq // tm
    blocks = _nbytes((tm, k), BF16) + _nbytes((k, tn), BF16) + _nbytes((tm, tn), BF16)
    return pl.pallas_call(
        functools.partial(_matmul_heads_kernel, heads_per_tile=hpt),
        grid=(m // tm, ncols // tn),
        in_specs=[pl.BlockSpec((tm, k), lambda i, j: (i, 0)),
                  pl.BlockSpec((k, tn), lambda i, j: (0, j + off))],
        out_specs=pl.BlockSpec((None, hpt, tm, HEAD_DIM), lambda i, j: (i // spt, j, i % spt, 0)),
        out_shape=jax.ShapeDtypeStruct((batch, ncols // HEAD_DIM, seq, HEAD_DIM), BF16),
        compiler_params=_params(("parallel", "arbitrary"), blocks, _nbytes((tm, tn), F32)),
        name="in_proj_heads",
    )(xn, w)


def _pool_kernel(a_ref, halo_ref, pw_ref, ps_ref, o_ref, ext_ref, sum_ref, *, seq, tm, group):
    t0 = lax.rem(pl.program_id(0) * tm, seq)
    a = a_ref[...].astype(F32)
    ext_ref[0:POOL_HALO, :] = jnp.where(t0 == 0, 0.0, halo_ref[...].astype(F32))
    ext_ref[POOL_HALO:POOL_HALO + tm, :] = a
    ext_ref[POOL_HALO + tm:, :] = jnp.zeros((POOL_HALO, a.shape[1]), F32)
    sum_ref[...] = jnp.zeros_like(sum_ref)
    pos = t0 + lax.broadcasted_iota(jnp.int32, (tm, 1), 0)
    n = tm + POOL_HALO + POOL_HALO // 2
    for g, w in enumerate(POOL_WINDOWS):
        cols = slice(g * group, (g + 1) * group)
        src, src_cols, dst, dst_cols, span = ext_ref, cols, sum_ref, slice(0, group), 1
        while span < w:
            dst[0:n, dst_cols] = src[0:n, src_cols] + src[span:span + n, src_cols]
            (src, src_cols), (dst, dst_cols) = (dst, dst_cols), (
                (sum_ref, slice(group, 2 * group)) if dst_cols.start == 0 else (sum_ref, slice(0, group)))
            span *= 2
        first = POOL_HALO - (w - 1)
        win = src[first:first + tm, src_cols]
        cnt = jnp.minimum(pos + 1, w).astype(F32)
        pooled = win / cnt - a[:, cols]
        y = jnp.dot(pooled.astype(BF16), pw_ref[g], preferred_element_type=F32)
        o_ref[:, cols] = (y * ps_ref[:, cols]).astype(o_ref.dtype)


def _pool_mixer(z_rows, pool_w, pool_scale, seq):
    m = z_rows.shape[0]
    n_groups, group, _ = pool_w.shape
    width = n_groups * group
    tm = _tile(seq, 1024, POOL_HALO)
    hpt = tm // POOL_HALO
    blocks = (_nbytes((tm, width), BF16) * 2 + _nbytes((POOL_HALO, width), BF16)
              + _nbytes(pool_w.shape, BF16) + _nbytes((1, width), F32))
    return pl.pallas_call(
        functools.partial(_pool_kernel, seq=seq, tm=tm, group=group),
        grid=(m // tm,),
        in_specs=[pl.BlockSpec((tm, width), lambda i: (i, 0)),
                  pl.BlockSpec((POOL_HALO, width), lambda i: (jnp.maximum(i * hpt - 1, 0), 0)),
                  pl.BlockSpec(pool_w.shape, lambda i: (0, 0, 0)),
                  pl.BlockSpec((1, width), lambda i: (0, 0))],
        out_specs=pl.BlockSpec((tm, width), lambda i: (i, 0)),
        out_shape=jax.ShapeDtypeStruct((m, width), BF16),
        scratch_shapes=[pltpu.VMEM((tm + 2 * POOL_HALO, width), F32),
                        pltpu.VMEM((tm + 2 * POOL_HALO, 2 * group), F32)],
        compiler_params=_params(("parallel",), blocks, 5 * _nbytes((tm + 2 * POOL_HALO, width), F32)),
        name="pool_mixer",
    )(z_rows, z_rows, pool_w, pool_scale)


def _gelu_tanh(x):
    c = math.sqrt(2.0 / math.pi)
    half = 0.5 * x
    return half + half * jnp.tanh(x * (c + (c * 0.044715) * (x * x)))


def _sgu_kernel(u_ref, v_ref, nw_ref, ws_ref, b_ref, o_ref, *, heads, chunks):
    v = _gelu_tanh(v_ref[...].astype(F32))
    vc = v - jnp.mean(v, axis=-1, keepdims=True)
    var = jnp.mean(vc * vc, axis=-1, keepdims=True)
    vn = (vc * lax.rsqrt(var + LN_EPS) * nw_ref[...]).astype(BF16)
    row = lax.broadcasted_iota(jnp.int32, (HEAD_DIM, HEAD_DIM), 0)
    col = lax.broadcasted_iota(jnp.int32, (HEAD_DIM, HEAD_DIM), 1)
    causal = row >= col
    for h in range(heads):
        cols = slice(h * HEAD_DIM, (h + 1) * HEAD_DIM)
        w = jnp.where(causal, ws_ref[h], 0.0).astype(BF16)
        bias = b_ref[:, h:h + 1]
        for c in range(chunks):
            rows = slice(c * HEAD_DIM, (c + 1) * HEAD_DIM)
            mixed = jnp.dot(w, vn[rows, cols], preferred_element_type=F32) + bias
            u = _gelu_tanh(u_ref[rows, cols].astype(F32))
            o_ref[rows, cols] = (u * mixed).astype(o_ref.dtype)


def _sgu_mixer(z_rows, col0, norm_w, w_s, b_t):
    m = z_rows.shape[0]
    heads = w_s.shape[0]
    width = heads * HEAD_DIM
    assert col0 % width == 0 and w_s.shape[1:] == (HEAD_DIM, HEAD_DIM)
    cb = col0 // width
    tm = _tile(m, 512, HEAD_DIM)
    blocks = (3 * _nbytes((tm, width), BF16) + _nbytes((1, width), F32) + _nbytes(w_s.shape, F32)
              + _nbytes((HEAD_DIM, LANES), F32))
    return pl.pallas_call(
        functools.partial(_sgu_kernel, heads=heads, chunks=tm // HEAD_DIM),
        grid=(m // tm,),
        in_specs=[pl.BlockSpec((tm, width), lambda i: (i, cb)),
                  pl.BlockSpec((tm, width), lambda i: (i, cb + 1)),
                  pl.BlockSpec((1, width), lambda i: (0, 0)),
                  pl.BlockSpec(w_s.shape, lambda i: (0, 0, 0)),
                  pl.BlockSpec(b_t.shape, lambda i: (0, 0))],
        out_specs=pl.BlockSpec((tm, width), lambda i: (i, 0)),
        out_shape=jax.ShapeDtypeStruct((m, width), BF16),
        compiler_params=_params(("parallel",), blocks, 4 * _nbytes((tm, width), F32)),
        name="sgu_mixer",
    )(z_rows, z_rows, norm_w, w_s, b_t)


def _neg_abs(x):
    bits = lax.bitcast_convert_type(x, jnp.uint32) | jnp.uint32(0x80000000)
    return lax.bitcast_convert_type(bits, F32)


def _bf16_split(x):
    bits = lax.bitcast_convert_type(x, jnp.uint32) & jnp.uint32(0xFFFF0000)
    hi = lax.bitcast_convert_type(bits, F32)
    return hi, x - hi


def _sb_streams(streams, cumsum_rhs, c2):
    scores = [lax.dot_general(q, kwin, (((1,), (1,)), ((), ())), preferred_element_type=F32)
              for q, kwin, _, _, _ in streams]
    log_betas, sums = [], []
    for s, (_, _, _, _, masks) in zip(scores, streams):
        z2 = s * c2
        sp = jnp.maximum(z2, 0.0) + jnp.log(1.0 + jnp.exp2(_neg_abs(z2))) * LOG2E
        log_betas.append(z2 - sp)
        lhs = []
        for j, mask in enumerate(masks):
            spj = sp[:, j * LANES:(j + 1) * LANES]
            if mask is not None:
                spj = jnp.where(mask, spj, 0.0)
            hi, lo = _bf16_split(spj)
            lhs.append(jnp.concatenate([hi.astype(BF16), lo.astype(BF16)], axis=1))
        lhs = lhs[0] if len(lhs) == 1 else jnp.concatenate(lhs, axis=0)
        sums.append(jnp.dot(lhs, cumsum_rhs, preferred_element_type=F32))
    out = []
    for (q, _, vwin, carry, masks), log_beta, both in zip(streams, log_betas, sums):
        r = q.shape[0]
        weights = [None] * len(masks)
        for j in reversed(range(len(masks))):
            inner = both[j * r:(j + 1) * r, :LANES]
            total = both[j * r:(j + 1) * r, LANES:]
            suffix = inner if carry is None else inner + carry
            pj = jnp.exp2(log_beta[:, j * LANES:(j + 1) * LANES] - suffix)
            if masks[j] is not None:
                pj = jnp.where(masks[j], pj, 0.0)
            weights[j] = pj.astype(BF16)
            carry = total if carry is None else carry + total
        p = weights[0] if len(weights) == 1 else jnp.concatenate(weights, axis=1)
        out.append((jnp.dot(p, vwin, preferred_element_type=F32), carry))
    return out


def _attn_kernel(q_ref, k_ref, v_ref, o_ref, *, heads, n_blocks, peel, group, c2):
    r = Q_ROWS
    krow = lax.broadcasted_iota(jnp.int32, (LANES, LANES), 0)
    kcol = lax.broadcasted_iota(jnp.int32, (LANES, LANES), 1)
    half = jnp.concatenate([(krow > kcol).astype(BF16), jnp.ones((LANES, LANES), BF16)], axis=1)
    cumsum_rhs = jnp.concatenate([half, half], axis=0)
    col = lax.broadcasted_iota(jnp.int32, (r, LANES), 1)
    ahead = col - lax.broadcasted_iota(jnp.int32, (r, LANES), 0)
    run = functools.partial(_sb_streams, cumsum_rhs=cumsum_rhs, c2=c2)

    def span(start, size):
        if isinstance(start, int):
            return slice(start, start + size)
        return pl.ds(pl.multiple_of(start, r), size)

    def run_blocks(blocks):
        def keys_back(b):
            return min(b * r, LEAD_BACK) if isinstance(b, int) else LEAD_BACK

        tasks = [(hh, b, keys_back(b)) for hh in range(heads) for b in blocks]
        streams = []
        for hh, b, back in tasks:
            n = -(-(back + r) // LANES)
            masks = [None if back - j * LANES >= LANES else ahead < back - j * LANES for j in range(n)]
            streams.append((q_ref[hh, span(b * r, r), :], k_ref[hh, span(b * r - back, n * LANES), :],
                            v_ref[hh, span(b * r - back, n * LANES), :], None, masks))
        done = run(streams)
        accs = tuple(acc for acc, _ in done)
        if any(not isinstance(b, int) or b * r - back > 0 for _, b, back in tasks):
            def left_edge(b, back, step):
                return b * r - back - step * LANES

            def cond(st):
                step, carries, _ = st
                least = None
                for (_, b, back), carry in zip(tasks, carries):
                    carry = carry + jnp.where(left_edge(b, back, step) <= 0, OUT_OF_KEYS_CARRY, 0.0)
                    least = carry if least is None else jnp.minimum(least, carry)
                return jnp.min(least) < UNDERFLOW_BITS

            def body(st):
                step, carries, accs = st
                more = []
                for (hh, b, back), carry in zip(tasks, carries):
                    edge = left_edge(b, back, step)
                    start = jnp.maximum(edge - LANES, 0)
                    carry = carry + jnp.where(edge <= 0, OUT_OF_KEYS_CARRY, 0.0)
                    more.append((q_ref[hh, span(b * r, r), :], k_ref[hh, span(start, LANES), :],
                                 v_ref[hh, span(start, LANES), :], carry, [col < edge - start]))
                parts = run(more)
                return (step + 1, tuple(carry for _, carry in parts),
                        tuple(acc + part for acc, (part, _) in zip(accs, parts)))

            _, _, accs = lax.while_loop(cond, body, (0, tuple(carry for _, carry in done), accs))
        for (hh, b, _), acc in zip(tasks, accs):
            o_ref[span(b * r, r), hh * HEAD_DIM:(hh + 1) * HEAD_DIM] = acc.astype(o_ref.dtype)

    for first in range(0, peel, group):
        run_blocks(list(range(first, min(first + group, peel))))

    def group_body(g, _):
        run_blocks([peel + g * group + t for t in range(group)])
        return 0

    if n_blocks > peel:
        lax.fori_loop(0, (n_blocks - peel) // group, group_body, 0)


def _sb_attention(z_heads, n_heads, batch, seq):
    heads = math.gcd(n_heads, 4)
    n_blocks = seq // Q_ROWS
    min_peel = min(-(-LEAD_BACK // Q_ROWS), n_blocks)
    group = min(4, max(n_blocks - min_peel, 1))
    peel = min_peel + (n_blocks - min_peel) % group
    hb = n_heads // heads
    blocks = 4 * _nbytes((heads, seq, HEAD_DIM), BF16)
    kern = functools.partial(_attn_kernel, heads=heads, n_blocks=n_blocks, peel=peel, group=group,
                             c2=HEAD_DIM ** -0.5 * LOG2E)
    return pl.pallas_call(
        kern,
        grid=(batch, hb),
        in_specs=[pl.BlockSpec((None, heads, seq, HEAD_DIM), lambda b, h: (b, h, 0, 0)),
                  pl.BlockSpec((None, heads, seq, HEAD_DIM), lambda b, h: (b, hb + h, 0, 0)),
                  pl.BlockSpec((None, heads, seq, HEAD_DIM), lambda b, h: (b, 2 * hb + h, 0, 0))],
        out_specs=pl.BlockSpec((None, seq, heads * HEAD_DIM), lambda b, h: (b, 0, h)),
        out_shape=jax.ShapeDtypeStruct((batch, seq, n_heads * HEAD_DIM), BF16),
        compiler_params=_params(("parallel", "parallel"), blocks, 8 * MIB),
        name="sb_attention",
    )(z_heads, z_heads, z_heads)


def _out_proj_kernel(yp_ref, ys_ref, yb_ref, w_ref, h_ref, g_ref, ho_ref, hg_ref, rs_ref,
                     ssq_ref, *, pw, sw, nj, d):
    j = pl.program_id(1)

    @pl.when(j == 0)
    def _():
        ssq_ref[...] = jnp.zeros_like(ssq_ref)

    acc = jnp.dot(yp_ref[...], w_ref[0:pw, :], preferred_element_type=F32)
    acc += jnp.dot(ys_ref[...], w_ref[pw:pw + sw, :], preferred_element_type=F32)
    acc += jnp.dot(yb_ref[...], w_ref[pw + sw:, :], preferred_element_type=F32)
    hn = h_ref[...] + acc
    ho_ref[...] = hn
    hg_ref[...] = (hn * g_ref[...]).astype(hg_ref.dtype)
    ssq_ref[...] += jnp.sum(hn * hn, axis=-1, keepdims=True)

    @pl.when(j == nj - 1)
    def _():
        rs_ref[...] = lax.rsqrt(ssq_ref[...] * (1.0 / d) + RMS_EPS)


def _out_proj(h, y_pool, y_sgu, y_sb, w_out, g_next):
    m, d = h.shape
    pw, sw, bw = y_pool.shape[1], y_sgu.shape[1], y_sb.shape[1]
    k = pw + sw + bw
    tm = _tile(m, 1024, 8)
    tn = _tile(d, 1024, LANES)
    nj = d // tn
    blocks = (_nbytes((tm, k), BF16) + _nbytes((k, tn), BF16) + 2 * _nbytes((tm, tn), F32)
              + _nbytes((8, tn), F32) + _nbytes((tm, tn), BF16) + _nbytes((tm, LANES), F32))
    return pl.pallas_call(
        functools.partial(_out_proj_kernel, pw=pw, sw=sw, nj=nj, d=d),
        grid=(m // tm, nj),
        in_specs=[pl.BlockSpec((tm, pw), lambda i, j: (i, 0)),
                  pl.BlockSpec((tm, sw), lambda i, j: (i, 0)),
                  pl.BlockSpec((tm, bw), lambda i, j: (i, 0)),
                  pl.BlockSpec((k, tn), lambda i, j: (0, j)),
                  pl.BlockSpec((tm, tn), lambda i, j: (i, j)),
                  pl.BlockSpec((1, tn), lambda i, j: (0, j))],
        out_specs=[pl.BlockSpec((tm, tn), lambda i, j: (i, j)),
                   pl.BlockSpec((tm, tn), lambda i, j: (i, j)),
                   pl.BlockSpec((tm, 1), lambda i, j: (i, 0))],
        out_shape=[jax.ShapeDtypeStruct((m, d), F32), jax.ShapeDtypeStruct((m, d), BF16),
                   jax.ShapeDtypeStruct((m, 1), F32)],
        scratch_shapes=[pltpu.VMEM((tm, 1), F32)],
        compiler_params=_params(("parallel", "arbitrary"), blocks,
                                _nbytes((tm, LANES), F32) + _nbytes((tm, tn), F32)),
        name="out_proj",
    )(y_pool, y_sgu, y_sb, w_out, h, g_next)


class _CastJobs:
    def __init__(self, jobs, grid):
        self.jobs = jobs
        self.plans = [self.plan(w.shape[1:], grid) for w, _ in jobs]
        assert all(p is not None for p in self.plans)

    @staticmethod
    def plan(shape, grid):
        rows, cols = shape
        for rows_on_first in (True, False):
            gr, gc = grid if rows_on_first else grid[::-1]
            if rows % gr == 0 and cols % gc == 0:
                block = (rows // gr, cols // gc)
                if (block[0] % BF16_SUBLANES == 0 and block[1] % LANES == 0
                        and _nbytes(block, F32) <= CAST_BLOCK_MAX_BYTES):
                    return block, rows_on_first
        return None

    def __len__(self):
        return len(self.jobs)

    def arrays(self):
        return [w for w, _ in self.jobs]

    def in_specs(self):
        return [pl.BlockSpec((None,) + block,
                             (lambda i, j, layer=layer: (layer, i, j)) if first else
                             (lambda i, j, layer=layer: (layer, j, i)))
                for (_, layer), (block, first) in zip(self.jobs, self.plans)]

    def out_specs(self):
        return [pl.BlockSpec(block, (lambda i, j: (i, j)) if first else (lambda i, j: (j, i)))
                for block, first in self.plans]

    def out_shapes(self):
        return [jax.ShapeDtypeStruct(w.shape[1:], BF16) for w, _ in self.jobs]

    def block_bytes(self):
        return sum(_nbytes(block, F32) + _nbytes(block, BF16) for block, _ in self.plans)


def _cast_kernel(w_ref, o_ref):
    o_ref[...] = w_ref[...].astype(o_ref.dtype)


def _cast_layer(w, layer):
    _, rows, cols = w.shape
    tr = _tile(rows, 128, BF16_SUBLANES)
    return pl.pallas_call(
        _cast_kernel,
        grid=(rows // tr,),
        in_specs=[pl.BlockSpec((None, tr, cols), lambda i: (layer, i, 0))],
        out_specs=pl.BlockSpec((tr, cols), lambda i: (i, 0)),
        out_shape=jax.ShapeDtypeStruct((rows, cols), BF16),
        compiler_params=_params(("parallel",), _nbytes((tr, cols), F32) + _nbytes((tr, cols), BF16)),
        name="cast_layer",
    )(w)


def _run_casts(refs, n):
    for src, dst in zip(refs[:n], refs[len(refs) - n:]):
        dst[...] = src[...].astype(dst.dtype)


def _gate_up_kernel(x_ref, rs_ref, wg_ref, wu_ref, *refs, chunks, n_casts):
    o_ref = refs[n_casts]
    rows_per_chunk = x_ref.shape[0] // chunks
    for c in range(chunks):
        rows = slice(c * rows_per_chunk, (c + 1) * rows_per_chunk)
        x = x_ref[rows, :]
        rs = rs_ref[rows, :]
        g = jnp.dot(x, wg_ref[...], preferred_element_type=F32) * rs
        u = jnp.dot(x, wu_ref[...], preferred_element_type=F32)
        o_ref[rows, :] = (g * jax.nn.sigmoid(g) * u).astype(o_ref.dtype)
    _run_casts(refs, n_casts)


def _gate_up_grid(m, f):
    tm = _tile(m, 2048, 8)
    tf = _tile(f, 512, LANES)
    return tm, tf, (m // tm, f // tf)


def _gate_up(hg, rs, w_gate, w_up, cast_jobs):
    m, k = hg.shape
    f = w_gate.shape[1]
    tm, tf, grid = _gate_up_grid(m, f)
    casts = _CastJobs(cast_jobs, grid)
    blocks = (_nbytes((tm, k), BF16) + _nbytes((tm, LANES), F32) + 2 * _nbytes((k, tf), BF16)
              + _nbytes((tm, tf), BF16) + casts.block_bytes())
    return pl.pallas_call(
        functools.partial(_gate_up_kernel, chunks=2 if tm % 1024 == 0 else 1, n_casts=len(casts)),
        grid=grid,
        in_specs=[pl.BlockSpec((tm, k), lambda i, j: (i, 0)),
                  pl.BlockSpec((tm, 1), lambda i, j: (i, 0)),
                  pl.BlockSpec((k, tf), lambda i, j: (0, j)),
                  pl.BlockSpec((k, tf), lambda i, j: (0, j))] + casts.in_specs(),
        out_specs=[pl.BlockSpec((tm, tf), lambda i, j: (i, j))] + casts.out_specs(),
        out_shape=[jax.ShapeDtypeStruct((m, f), BF16)] + casts.out_shapes(),
        compiler_params=_params(("parallel", "arbitrary"), blocks, 3 * _nbytes((tm, tf), F32)),
        name="ffn_gate_up",
    )(hg, rs, w_gate, w_up, *casts.arrays())


def _down_kernel(a_ref, w_ref, h_ref, rs_ref, *refs, n_casts):
    o_ref = refs[n_casts]
    o_ref[...] = h_ref[...] + rs_ref[...] * jnp.dot(a_ref[...], w_ref[...], preferred_element_type=F32)
    _run_casts(refs, n_casts)


def _down_grid(m, d):
    tm = _tile(m, 512, 8)
    tn = _tile(d, 1024, LANES)
    return tm, tn, (d // tn, m // tm)


def _down_proj(h, act, rs, w_down, cast_jobs):
    m, d = h.shape
    f = act.shape[1]
    tm, tn, grid = _down_grid(m, d)
    casts = _CastJobs(cast_jobs, grid)
    blocks = (_nbytes((tm, f), BF16) + 2 * _nbytes((tm, tn), F32) + _nbytes((tm, LANES), F32)
              + casts.block_bytes())
    return pl.pallas_call(
        functools.partial(_down_kernel, n_casts=len(casts)),
        grid=grid,
        in_specs=[pl.BlockSpec((tm, f), lambda j, i: (i, 0)),
                  pl.BlockSpec((f, tn), lambda j, i: (0, j), pipeline_mode=pl.Buffered(1)),
                  pl.BlockSpec((tm, tn), lambda j, i: (i, j)),
                  pl.BlockSpec((tm, 1), lambda j, i: (i, 0))] + casts.in_specs(),
        out_specs=[pl.BlockSpec((tm, tn), lambda j, i: (i, j))] + casts.out_specs(),
        out_shape=[jax.ShapeDtypeStruct((m, d), F32)] + casts.out_shapes(),
        compiler_params=_params(("arbitrary", "arbitrary"), blocks,
                                _nbytes((f, tn), BF16) + _nbytes((tm, tn), F32)),
        name="ffn_down",
    )(act, w_down, h, rs, *casts.arrays())


def _ple_kernel(h_ref, p_ref, gn_ref, gd_ref, gu_ref, pp_ref, gnext_ref, *out_refs):
    h = h_ref[...]
    hg = (h * gn_ref[...]).astype(BF16)
    low = (jnp.dot(hg, gd_ref[...], preferred_element_type=F32) * _rms_scale(h)).astype(BF16)
    gate = jax.nn.sigmoid(jnp.dot(low, gu_ref[...], preferred_element_type=F32))
    emb = jnp.dot(p_ref[...].astype(BF16), pp_ref[...], preferred_element_type=F32)
    hn = h + gate * emb
    xn_ref = out_refs[-1]
    if len(out_refs) == 2:
        out_refs[0][...] = hn
    xn_ref[...] = (hn * _rms_scale(hn) * gnext_ref[...]).astype(xn_ref.dtype)


def _ple(h, p, layer, g_ple, gate_down, gate_up, proj, g_next, last):
    m, d = h.shape
    e = p.shape[2]
    tm = _tile(m, 256, 8)
    row = pl.BlockSpec((tm, d), lambda i: (i, 0))
    vec = pl.BlockSpec((1, d), lambda i: (0, 0))
    if last:
        out_specs = [row]
        out_shape = [jax.ShapeDtypeStruct((m, d), F32)]
        out_bytes = _nbytes((tm, d), F32)
    else:
        out_specs = [row, row]
        out_shape = [jax.ShapeDtypeStruct((m, d), F32), jax.ShapeDtypeStruct((m, d), BF16)]
        out_bytes = _nbytes((tm, d), F32) + _nbytes((tm, d), BF16)
    blocks = (_nbytes((tm, d), F32) + _nbytes((tm, e), F32) + 2 * _nbytes((1, d), F32)
              + 3 * _nbytes((d, e), BF16) + out_bytes)
    return pl.pallas_call(
        _ple_kernel,
        grid=(m // tm,),
        in_specs=[row,
                  pl.BlockSpec((None, tm, e), lambda i: (layer, i, 0)),
                  vec,
                  pl.BlockSpec((d, e), lambda i: (0, 0)),
                  pl.BlockSpec((e, d), lambda i: (0, 0)),
                  pl.BlockSpec((e, d), lambda i: (0, 0)),
                  vec],
        out_specs=out_specs,
        out_shape=out_shape,
        compiler_params=_params(("parallel",), blocks, 4 * _nbytes((tm, d), F32)),
        name="ple_gate",
    )(h, p, g_ple, gate_down, gate_up, proj, g_next)


def kernel(x, p, norm_mix_w, w_in, pool_w, pool_scale, sgu_norm_w, sgu_w, sgu_b, w_out, norm_ffn_w,
           w_gate, w_up, w_down, norm_ple_w, ple_gate_down, ple_gate_up, ple_proj, final_norm_w):
    batch, seq, d = x.shape
    depth = w_in.shape[0]
    m = batch * seq
    pool_width = pool_scale.shape[-1]
    sgu_width = sgu_norm_w.shape[-1]
    sb_width = w_out.shape[1] - pool_width - sgu_width
    n_heads = sb_width // HEAD_DIM
    rows_width = pool_width + 2 * sgu_width
    assert w_in.shape[2] == rows_width + 3 * sb_width and sb_width % HEAD_DIM == 0
    assert seq % LANES == 0 and sgu_w.shape[1] == sgu_width // HEAD_DIM
    assert pool_w.shape[1] == len(POOL_WINDOWS) and pool_w.shape[1] * pool_w.shape[2] == pool_width

    f = w_gate.shape[2]
    gate_up_grid, down_grid = _gate_up_grid(m, f)[2], _down_grid(m, d)[2]
    ride_gate_up = all(_CastJobs.plan(w.shape[1:], gate_up_grid) for w in (w_down, w_gate, w_up))
    ride_down = all(_CastJobs.plan(w.shape[1:], down_grid) for w in (w_in, w_out))

    w_in_b, w_out_b, w_gate_b, w_up_b = (_cast_layer(w, 0) for w in (w_in, w_out, w_gate, w_up))
    h = x.reshape(m, d)
    p_rows = p.reshape(depth, m, p.shape[-1])
    xn = _rmsnorm(h, norm_mix_w[0].reshape(1, d))
    out = None
    for i in range(depth):
        more = i + 1 < depth
        z_rows = _in_proj_rows(xn, w_in_b, 0, rows_width)
        z_heads = _in_proj_heads(xn, w_in_b, rows_width, 3 * sb_width, batch, seq)
        y_pool = _pool_mixer(z_rows, pool_w[i].astype(BF16), pool_scale[i].reshape(1, pool_width), seq)
        y_sgu = _sgu_mixer(z_rows, pool_width, sgu_norm_w[i].reshape(1, sgu_width), sgu_w[i],
                           sgu_b[i].T)
        y_sb = _sb_attention(z_heads, n_heads, batch, seq).reshape(m, sb_width)
        h, hg, rs = _out_proj(h, y_pool, y_sgu, y_sb, w_out_b, norm_ffn_w[i].reshape(1, d))

        jobs = [(w_down, i)] + ([(w_gate, i + 1), (w_up, i + 1)] if more else [])
        act, *cast = _gate_up(hg, rs, w_gate_b, w_up_b, jobs if ride_gate_up else [])
        if ride_gate_up:
            w_down_b = cast[0]
            if more:
                w_gate_b, w_up_b = cast[1], cast[2]
        else:
            w_down_b = _cast_layer(w_down, i)
            if more:
                w_gate_b, w_up_b = _cast_layer(w_gate, i + 1), _cast_layer(w_up, i + 1)

        jobs = [(w_in, i + 1), (w_out, i + 1)] if more and ride_down else []
        h, *cast = _down_proj(h, act, rs, w_down_b, jobs)
        if jobs:
            w_in_b, w_out_b = cast
        elif more:
            w_in_b, w_out_b = _cast_layer(w_in, i + 1), _cast_layer(w_out, i + 1)
        last = i == depth - 1
        g_next = final_norm_w if last else norm_mix_w[i + 1]
        res = _ple(h, p_rows, i, norm_ple_w[i].reshape(1, d), ple_gate_down[i].astype(BF16),
                   ple_gate_up[i].astype(BF16), ple_proj[i].astype(BF16), g_next.reshape(1, d), last)
        if last:
            out = res[0]
        else:
            h, xn = res
    return out.reshape(batch, seq, d)
```
